```python
import jax, jax.numpy as jnp
from jax import lax
import numpy as np

D_MODEL = 1024
BATCH = 4
SEQ = 4096
DEPTH = 1
DEC_BATCH = 16
DEC_SEQ = 16
PAST_LEN = 4096

CHUNK = 64
D_MIX = D_MODEL
D_POOL = D_MIX // 2
D_CONV = D_MIX - D_POOL
POOL_WINDOWS = (2, 4, 8, 16)
N_POOL_GROUPS = len(POOL_WINDOWS)
POOL_GROUP = D_POOL // N_POOL_GROUPS
POOL_HIST = max(POOL_WINDOWS) - 1
CONV_WIDTH = 31
CONV_HIST = CONV_WIDTH - 1
D_IN = 2 * D_POOL + 3 * D_CONV
RMS_EPS = 1e-6
LN_EPS = 1e-5

kernel_name = "hymba_pool_conformer_stream"


def rms_norm(x, g):
    x32 = x.astype(jnp.float32)
    y = x32 * lax.rsqrt(jnp.mean(x32 * x32, axis=-1, keepdims=True) + RMS_EPS)
    return (y * g.astype(jnp.float32)).astype(x.dtype)


def layer_norm(x, g, b):
    x32 = x.astype(jnp.float32)
    mu = jnp.mean(x32, axis=-1, keepdims=True)
    xc = x32 - mu
    var = jnp.mean(xc * xc, axis=-1, keepdims=True)
    y = xc * lax.rsqrt(var + LN_EPS) * g.astype(jnp.float32) + b.astype(jnp.float32)
    return y.astype(x.dtype)


def multiscale_pool(u_ext, p0, t_new):
    u32 = u_ext.astype(jnp.float32)
    cs = jnp.cumsum(u32, axis=1)
    cs = jnp.concatenate([jnp.zeros_like(cs[:, :1]), cs], axis=1)
    end = cs[:, POOL_HIST + 1:POOL_HIST + 1 + t_new]
    cur = u32[:, POOL_HIST:]
    pos = p0 + jnp.arange(t_new, dtype=jnp.int32)
    outs = []
    for gi, w in enumerate(POOL_WINDOWS):
        sl = slice(gi * POOL_GROUP, (gi + 1) * POOL_GROUP)
        start = cs[:, POOL_HIST + 1 - w:POOL_HIST + 1 - w + t_new, sl]
        cnt = jnp.minimum(pos + 1, w).astype(jnp.float32)[None, :, None]
        outs.append((end[..., sl] - start) / cnt - cur[..., sl])
    return jnp.concatenate(outs, axis=-1).astype(u_ext.dtype)


def causal_dwconv(v_ext, w, b):
    y = lax.conv_general_dilated(
        v_ext, w[:, None, :].astype(v_ext.dtype), window_strides=(1,), padding='VALID',
        dimension_numbers=('NWC', 'WIO', 'NWC'), feature_group_count=v_ext.shape[-1])
    return y + b


def mixer_layer(x, pool_hist, conv_hist, p0, norm_g, w_in, pool_mix, pool_scale,
                dw_w, dw_b, ln_g, ln_b, pw_w, pw_b, w_out):
    B, T, _ = x.shape
    h = rms_norm(x, norm_g)
    proj = jnp.einsum('btd,de->bte', h, w_in)
    u, g_pool, v_a, v_b, g_conv = jnp.split(
        proj, [D_POOL, 2 * D_POOL, 2 * D_POOL + D_CONV, 2 * D_POOL + 2 * D_CONV], axis=-1)
    u_ext = jnp.concatenate([pool_hist.astype(u.dtype), u], axis=1)
    pooled = multiscale_pool(u_ext, p0, T)
    pooled = jnp.einsum('btgc,gcd->btgd', pooled.reshape(B, T, N_POOL_GROUPS, POOL_GROUP),
                        pool_mix).reshape(B, T, D_POOL) * pool_scale
    pool_out = jax.nn.silu(g_pool) * pooled
    v = v_a * jax.nn.sigmoid(v_b)
    v_ext = jnp.concatenate([conv_hist.astype(v.dtype), v], axis=1)
    c = causal_dwconv(v_ext, dw_w, dw_b)
    c = jax.nn.silu(layer_norm(c, ln_g, ln_b))
    c = jnp.einsum('btc,ce->bte', c, pw_w) + pw_b
    conv_out = jax.nn.silu(g_conv) * c
    y = x + jnp.einsum('bte,ed->btd', jnp.concatenate([pool_out, conv_out], axis=-1), w_out)
    return y, u_ext[:, -POOL_HIST:], v_ext[:, -CONV_HIST:]


def setup_inputs(seed: int = 0) -> dict:
    key = jax.random.key(seed)
    ks = jax.random.split(key, 16)
    f32 = jnp.float32
    nrm = lambda k, s: jax.random.normal(k, s, f32)
    return {
        'x_prompt': nrm(ks[0], (BATCH, SEQ, D_MODEL)),
        'x_sample': nrm(ks[1], (DEC_BATCH, DEC_SEQ, D_MODEL)),
        'cache_pool': nrm(ks[2], (DEPTH, DEC_BATCH, POOL_HIST, D_POOL)),
        'cache_conv': 0.5 * nrm(ks[3], (DEPTH, DEC_BATCH, CONV_HIST, D_CONV)),
        'norm_g': 1.0 + 0.05 * nrm(ks[4], (DEPTH, D_MODEL)),
        'w_in': nrm(ks[5], (DEPTH, D_MODEL, D_IN)) * D_MODEL ** -0.5,
        'pool_mix': nrm(ks[6], (DEPTH, N_POOL_GROUPS, POOL_GROUP, POOL_GROUP)) * POOL_GROUP ** -0.5,
        'pool_scale': 1.0 + 0.1 * nrm(ks[7], (DEPTH, D_POOL)),
        'dw_w': nrm(ks[8], (DEPTH, CONV_WIDTH, D_CONV)) * CONV_WIDTH ** -0.5,
        'dw_b': 0.02 * nrm(ks[9], (DEPTH, D_CONV)),
        'ln_g': 1.0 + 0.05 * nrm(ks[10], (DEPTH, D_CONV)),
        'ln_b': 0.02 * nrm(ks[11], (DEPTH, D_CONV)),
        'pw_w': nrm(ks[12], (DEPTH, D_CONV, D_CONV)) * D_CONV ** -0.5,
        'pw_b': 0.02 * nrm(ks[13], (DEPTH, D_CONV)),
        'w_out': nrm(ks[14], (DEPTH, D_MIX, D_MODEL)) * D_MIX ** -0.5,
        'final_g': 1.0 + 0.05 * nrm(ks[15], (D_MODEL,)),
    }


def reference(x_prompt, x_sample, cache_pool, cache_conv, norm_g, w_in, pool_mix, pool_scale,
              dw_w, dw_b, ln_g, ln_b, pw_w, pw_b, w_out, final_g):
    yp = x_prompt
    ys = x_sample
    bp = x_prompt.shape[0]
    sp_pool, sp_conv, ss_pool, ss_conv = [], [], [], []
    for l in range(DEPTH):
        lw = (norm_g[l], w_in[l], pool_mix[l], pool_scale[l], dw_w[l], dw_b[l],
              ln_g[l], ln_b[l], pw_w[l], pw_b[l], w_out[l])
        zero_pool = jnp.zeros((bp, POOL_HIST, D_POOL), x_prompt.dtype)
        zero_conv = jnp.zeros((bp, CONV_HIST, D_CONV), x_prompt.dtype)
        yp, hp_pool, hp_conv = mixer_layer(yp, zero_pool, zero_conv, 0, *lw)
        ys, hs_pool, hs_conv = mixer_layer(ys, cache_pool[l], cache_conv[l], PAST_LEN, *lw)
        sp_pool.append(hp_pool)
        sp_conv.append(hp_conv)
        ss_pool.append(hs_pool)
        ss_conv.append(hs_conv)
    y_prompt = rms_norm(yp, final_g)
    y_sample = rms_norm(ys, final_g)
    return (y_prompt, y_sample, jnp.stack(sp_pool), jnp.stack(sp_conv),
            jnp.stack(ss_pool), jnp.stack(ss_conv))
```

```python
import functools

import jax
import jax.numpy as jnp
from jax import lax
from jax.experimental import pallas as pl
from jax.experimental.pallas import tpu as pltpu

D_MODEL = 1024
D_POOL = 512
D_CONV = 512
POOL_WINDOWS = (2, 4, 8, 16)
POOL_GROUP = 128
POOL_HIST = 15
CONV_WIDTH = 31
CONV_HIST = 30
D_IN = 2 * D_POOL + 3 * D_CONV
RMS_EPS = 1e-6
LN_EPS = 1e-5
PAST_LEN = 4096

LANES = 128
POOL_PAD = 16
CONV_PAD = 32
TIME_TILE = 512
ROW_CHUNK = 64
VMEM_LIMIT_BYTES = 56 * 1024 * 1024

F32 = jnp.float32
BF16 = jnp.bfloat16


def _rms_norm(x, g):
    ms = jnp.mean(x * x, axis=-1, keepdims=True)
    return x * lax.rsqrt(ms + RMS_EPS) * g


def _silu(x):
    return x * jax.nn.sigmoid(x)


def _pool_rows(uext, row0, rows, cnt=None):
    outs = []
    for gi, w in enumerate(POOL_WINDOWS):
        sl = slice(gi * POOL_GROUP, (gi + 1) * POOL_GROUP)
        cur = uext[pl.ds(row0, rows), sl]
        acc = cur
        for i in range(1, w):
            acc = acc + uext[pl.ds(row0 - i, rows), sl]
        if cnt is None:
            outs.append(acc * (1.0 / w) - cur)
        else:
            outs.append(acc / cnt(w) - cur)
    return jnp.concatenate(outs, axis=-1)


def _conv_rows(vext, row0, rows, dww_ref, dwb_ref):
    outs = []
    for cb in range(D_CONV // LANES):
        sl = slice(cb * LANES, (cb + 1) * LANES)
        acc = jnp.broadcast_to(dwb_ref[:, sl], (rows, LANES))
        for k in range(CONV_WIDTH):
            acc = acc + vext[pl.ds(row0 - CONV_HIST + k, rows), sl] * dww_ref[k:k + 1, sl]
        outs.append(acc)
    return jnp.concatenate(outs, axis=-1)


def _conv_post(c, lng, lnb):
    mu = jnp.mean(c, axis=-1, keepdims=True)
    xc = c - mu
    var = jnp.mean(xc * xc, axis=-1, keepdims=True)
    return _silu(xc * lax.rsqrt(var + LN_EPS) * lng + lnb)


def _project(h, win_ref, lo, hi):
    return jnp.dot(h, win_ref[:, lo:hi], preferred_element_type=F32)


def _pool_mix(pooled_ref, bd_ref):
    half = 2 * POOL_GROUP
    return jnp.concatenate(
        [jnp.dot(pooled_ref[:, i * half:(i + 1) * half], bd_ref[i], preferred_element_type=F32)
         for i in range(D_POOL // half)], axis=-1)


def _prompt_kernel(x_ref, ng_ref, win_ref, bd_ref, ps_ref, dww_ref, dwb_ref, lng_ref, lnb_ref,
                   pww_ref, pwb_ref, wout_ref, fg_ref,
                   y_ref, sp_ref, sc_ref,
                   uext_ref, vext_ref, pooled_ref, act_ref, cat_ref):
    j = pl.program_id(1)
    tt = TIME_TILE

    @pl.when(j == 0)
    def _():
        uext_ref[0:POOL_PAD, :] = jnp.zeros((POOL_PAD, D_POOL), F32)
        vext_ref[0:CONV_PAD, :] = jnp.zeros((CONV_PAD, D_CONV), F32)

    x = x_ref[...]
    h = _rms_norm(x, ng_ref[...]).astype(BF16)
    uext_ref[POOL_PAD:POOL_PAD + tt, :] = _project(h, win_ref, 0, D_POOL)
    g_pool = _project(h, win_ref, D_POOL, 2 * D_POOL)
    v_a = _project(h, win_ref, 2 * D_POOL, 2 * D_POOL + D_CONV)
    v_b = _project(h, win_ref, 2 * D_POOL + D_CONV, 2 * D_POOL + 2 * D_CONV)
    vext_ref[CONV_PAD:CONV_PAD + tt, :] = v_a * jax.nn.sigmoid(v_b)
    g_conv = _project(h, win_ref, 2 * D_POOL + 2 * D_CONV, D_IN)

    row = lax.broadcasted_iota(jnp.int32, (ROW_CHUNK, LANES), 0)

    def first_cnt(w):
        return jnp.where(j == 0, jnp.minimum(row + 1, w), w).astype(F32)

    for r0 in range(0, tt, ROW_CHUNK):
        pooled = _pool_rows(uext_ref, POOL_PAD + r0, ROW_CHUNK, first_cnt if r0 == 0 else None)
        pooled_ref[r0:r0 + ROW_CHUNK, :] = pooled.astype(BF16)
        c = _conv_rows(vext_ref, CONV_PAD + r0, ROW_CHUNK, dww_ref, dwb_ref)
        act_ref[r0:r0 + ROW_CHUNK, :] = _conv_post(c, lng_ref[...], lnb_ref[...]).astype(BF16)

    pool_out = _silu(g_pool) * (_pool_mix(pooled_ref, bd_ref) * ps_ref[...])
    cat_ref[:, 0:D_POOL] = pool_out.astype(BF16)
    cmix = jnp.dot(act_ref[...], pww_ref[...], preferred_element_type=F32) + pwb_ref[...]
    cat_ref[:, D_POOL:D_MODEL] = (_silu(g_conv) * cmix).astype(BF16)

    y = x + jnp.dot(cat_ref[...], wout_ref[...], preferred_element_type=F32)
    y_ref[...] = _rms_norm(y, fg_ref[...])

    @pl.when(j == pl.num_programs(1) - 1)
    def _():
        sp_ref[...] = uext_ref[POOL_PAD + tt - POOL_HIST:POOL_PAD + tt, :]
        sc_ref[...] = vext_ref[CONV_PAD + tt - CONV_HIST:CONV_PAD + tt, :]

    uext_ref[0:POOL_PAD, :] = uext_ref[tt:tt + POOL_PAD, :]
    vext_ref[0:CONV_PAD, :] = vext_ref[tt:tt + CONV_PAD, :]


def _sample_kernel(x_ref, cp_ref, cc_ref, ng_ref, win_ref, bd_ref, ps_ref, dww_ref, dwb_ref, lng_ref,
                   lnb_ref, pww_ref, pwb_ref, wout_ref, fg_ref,
                   y_ref, sp_ref, sc_ref,
                   uext_ref, vext_ref, pooled_ref, act_ref, cat_ref, *, n_streams, seq):
    x = x_ref[...]
    h = _rms_norm(x, ng_ref[...]).astype(BF16)
    u = _project(h, win_ref, 0, D_POOL)
    g_pool = _project(h, win_ref, D_POOL, 2 * D_POOL)
    v_a = _project(h, win_ref, 2 * D_POOL, 2 * D_POOL + D_CONV)
    v_b = _project(h, win_ref, 2 * D_POOL + D_CONV, 2 * D_POOL + 2 * D_CONV)
    v = v_a * jax.nn.sigmoid(v_b)
    g_conv = _project(h, win_ref, 2 * D_POOL + 2 * D_CONV, D_IN)

    for s in range(n_streams):
        ue = uext_ref.at[s]
        ve = vext_ref.at[s]
        ue[POOL_PAD - POOL_HIST:POOL_PAD, :] = cp_ref[s]
        ue[POOL_PAD:POOL_PAD + seq, :] = u[s * seq:(s + 1) * seq, :]
        ve[CONV_PAD - CONV_HIST:CONV_PAD, :] = cc_ref[s]
        ve[CONV_PAD:CONV_PAD + seq, :] = v[s * seq:(s + 1) * seq, :]
        pooled_ref[s * seq:(s + 1) * seq, :] = _pool_rows(ue, POOL_PAD, seq).astype(BF16)
        c = _conv_rows(ve, CONV_PAD, seq, dww_ref, dwb_ref)
        act_ref[s * seq:(s + 1) * seq, :] = _conv_post(c, lng_ref[...], lnb_ref[...]).astype(BF16)
        sp_ref[s] = ue[POOL_PAD + seq - POOL_HIST:POOL_PAD + seq, :]
        sc_ref[s] = ve[CONV_PAD + seq - CONV_HIST:CONV_PAD + seq, :]

    pool_out = _silu(g_pool) * (_pool_mix(pooled_ref, bd_ref) * ps_ref[...])
    cat_ref[:, 0:D_POOL] = pool_out.astype(BF16)
    cmix = jnp.dot(act_ref[...], pww_ref[...], preferred_element_type=F32) + pwb_ref[...]
    cat_ref[:, D_POOL:D_MODEL] = (_silu(g_conv) * cmix).astype(BF16)

    y = x + jnp.dot(cat_ref[...], wout_ref[...], preferred_element_type=F32)
    y_ref[...] = _rms_norm(y, fg_ref[...])


def _full(shape):
    return pl.BlockSpec(shape, lambda *_: (0,) * len(shape))


def _weight_specs():
    return [
        _full((1, D_MODEL)),
        _full((D_MODEL, D_IN)),
        _full((2, 2 * POOL_GROUP, 2 * POOL_GROUP)),
        _full((1, D_POOL)),
        _full((CONV_WIDTH, D_CONV)),
        _full((1, D_CONV)),
        _full((1, D_CONV)),
        _full((1, D_CONV)),
        _full((D_CONV, D_CONV)),
        _full((1, D_CONV)),
        _full((D_MODEL, D_MODEL)),
        _full((1, D_MODEL)),
    ]


def kernel(x_prompt, x_sample, cache_pool, cache_conv, norm_g, w_in, pool_mix, pool_scale, dw_w, dw_b,
           ln_g, ln_b, pw_w, pw_b, w_out, final_g):
    batch, seq, d_model = x_prompt.shape
    dec_batch, dec_seq, _ = x_sample.shape
    assert d_model == D_MODEL and w_in.shape == (1, D_MODEL, D_IN)
    assert seq % TIME_TILE == 0 and dec_seq % 8 == 0

    pm = pool_mix[0].astype(BF16)
    zero = jnp.zeros((POOL_GROUP, POOL_GROUP), BF16)
    bd = jnp.stack([jnp.block([[pm[2 * i], zero], [zero, pm[2 * i + 1]]]) for i in range(2)])
    weights = (norm_g, w_in[0].astype(BF16), bd, pool_scale, dw_w[0], dw_b, ln_g, ln_b,
               pw_w[0].astype(BF16), pw_b, w_out[0].astype(BF16), final_g.reshape(1, D_MODEL))

    n_tiles = seq // TIME_TILE
    y_prompt, sp_prompt, sc_prompt = pl.pallas_call(
        _prompt_kernel,
        grid=(batch, n_tiles),
        in_specs=[pl.BlockSpec((None, TIME_TILE, D_MODEL), lambda b, j: (b, j, 0))] + _weight_specs(),
        out_specs=[
            pl.BlockSpec((None, TIME_TILE, D_MODEL), lambda b, j: (b, j, 0)),
            pl.BlockSpec((None, None, POOL_HIST, D_POOL), lambda b, j: (0, b, 0, 0)),
            pl.BlockSpec((None, None, CONV_HIST, D_CONV), lambda b, j: (0, b, 0, 0)),
        ],
        out_shape=[
            jax.ShapeDtypeStruct((batch, seq, D_MODEL), F32),
            jax.ShapeDtypeStruct((1, batch, POOL_HIST, D_POOL), F32),
            jax.ShapeDtypeStruct((1, batch, CONV_HIST, D_CONV), F32),
        ],
        scratch_shapes=[
            pltpu.VMEM((POOL_PAD + TIME_TILE, D_POOL), F32),
            pltpu.VMEM((CONV_PAD + TIME_TILE, D_CONV), F32),
            pltpu.VMEM((TIME_TILE, D_POOL), BF16),
            pltpu.VMEM((TIME_TILE, D_CONV), BF16),
            pltpu.VMEM((TIME_TILE, D_MODEL), BF16),
        ],
        compiler_params=pltpu.CompilerParams(
            dimension_semantics=("arbitrary", "arbitrary"), vmem_limit_bytes=VMEM_LIMIT_BYTES),
        name="prompt_mixer",
    )(x_prompt, *weights)

    rows = dec_batch * dec_seq
    y_sample, sp_sample, sc_sample = pl.pallas_call(
        functools.partial(_sample_kernel, n_streams=dec_batch, seq=dec_seq),
        grid=(1,),
        in_specs=[_full((rows, D_MODEL)), _full((dec_batch, POOL_HIST, D_POOL)),
                  _full((dec_batch, CONV_HIST, D_CONV))] + _weight_specs(),
        out_specs=[_full((rows, D_MODEL)), _full((dec_batch, POOL_HIST, D_POOL)),
                   _full((dec_batch, CONV_HIST, D_CONV))],
        out_shape=[
            jax.ShapeDtypeStruct((rows, D_MODEL), F32),
            jax.ShapeDtypeStruct((dec_batch, POOL_HIST, D_POOL), F32),
            jax.ShapeDtypeStruct((dec_batch, CONV_HIST, D_CONV), F32),
        ],
        scratch_shapes=[
            pltpu.VMEM((dec_batch, POOL_PAD + dec_seq, D_POOL), F32),
            pltpu.VMEM((dec_batch, CONV_PAD + dec_seq, D_CONV), F32),
            pltpu.VMEM((rows, D_POOL), BF16),
            pltpu.VMEM((rows, D_CONV), BF16),
            pltpu.VMEM((rows, D_MODEL), BF16),
        ],
        compiler_params=pltpu.CompilerParams(
            dimension_semantics=("arbitrary",), vmem_limit_bytes=VMEM_LIMIT_BYTES),
        name="sample_mixer",
    )(x_sample.reshape(rows, D_MODEL), cache_pool[0], cache_conv[0], *weights)

    return (y_prompt, y_sample.reshape(dec_batch, dec_seq, D_MODEL), sp_prompt, sc_prompt,
            sp_sample[None], sc_sample[None])
```

```python
import functools

import jax
import jax.numpy as jnp
from jax import lax
from jax.experimental import pallas as pl
from jax.experimental.pallas import tpu as pltpu

D_MODEL = 1024
D_POOL = 512
D_CONV = 512
POOL_WINDOWS = (2, 4, 8, 16)
POOL_GROUP = 128
POOL_HIST = 15
CONV_WIDTH = 31
CONV_HIST = 30
D_IN = 2 * D_POOL + 3 * D_CONV
RMS_EPS = 1e-6
LN_EPS = 1e-5
PAST_LEN = 4096

LANES = 128
POOL_PAD = 16
CONV_PAD = 32
TIME_TILE = 512
ROW_CHUNK = 64
VMEM_LIMIT_BYTES = 56 * 1024 * 1024

F32 = jnp.float32
BF16 = jnp.bfloat16


def _rms_norm(x, g):
    ms = jnp.mean(x * x, axis=-1, keepdims=True)
    return x * lax.rsqrt(ms + RMS_EPS) * g


def _silu(x):
    return x * jax.nn.sigmoid(x)


def _store_blocked(ext, row0, val):
    for cb in range(val.shape[-1] // LANES):
        ext[cb, pl.ds(row0, val.shape[0]), :] = val[:, cb * LANES:(cb + 1) * LANES]


def _load_blocked(ext, row0, rows):
    return jnp.concatenate([ext[cb, pl.ds(row0, rows), :] for cb in range(ext.shape[0])], axis=-1)


def _pool_rows(uext, row0, rows, cnt=None):
    outs = []
    for gi, w in enumerate(POOL_WINDOWS):
        cur = uext[gi, pl.ds(row0, rows), :]
        acc = cur
        for i in range(1, w):
            acc = acc + uext[gi, pl.ds(row0 - i, rows), :]
        if cnt is None:
            outs.append(acc * (1.0 / w) - cur)
        else:
            outs.append(acc / cnt(w) - cur)
    return jnp.concatenate(outs, axis=-1)


def _conv_rows(vext, row0, rows, dww_ref, dwb_ref):
    outs = []
    for cb in range(D_CONV // LANES):
        sl = slice(cb * LANES, (cb + 1) * LANES)
        acc = jnp.broadcast_to(dwb_ref[:, sl], (rows, LANES))
        for k in range(CONV_WIDTH):
            acc = acc + vext[cb, pl.ds(row0 - CONV_HIST + k, rows), :] * dww_ref[k:k + 1, sl]
        outs.append(acc)
    return jnp.concatenate(outs, axis=-1)


def _conv_post(c, lng, lnb):
    mu = jnp.mean(c, axis=-1, keepdims=True)
    xc = c - mu
    var = jnp.mean(xc * xc, axis=-1, keepdims=True)
    return _silu(xc * lax.rsqrt(var + LN_EPS) * lng + lnb)


def _project(h, win_ref, lo, hi):
    return jnp.dot(h, win_ref[:, lo:hi], preferred_element_type=F32)


def _pool_mix(pooled_ref, bd_ref):
    half = 2 * POOL_GROUP
    return jnp.concatenate(
        [jnp.dot(pooled_ref[:, i * half:(i + 1) * half], bd_ref[i], preferred_element_type=F32)
         for i in range(D_POOL // half)], axis=-1)


def _prompt_kernel(x_ref, ng_ref, win_ref, bd_ref, ps_ref, dww_ref, dwb_ref, lng_ref, lnb_ref,
                   pww_ref, pwb_ref, wout_ref, fg_ref,
                   y_ref, sp_ref, sc_ref,
                   uext_ref, vext_ref, pooled_ref, act_ref, cat_ref):
    j = pl.program_id(1)
    tt = TIME_TILE

    @pl.when(j == 0)
    def _():
        _store_blocked(uext_ref, 0, jnp.zeros((POOL_PAD, D_POOL), F32))
        _store_blocked(vext_ref, 0, jnp.zeros((CONV_PAD, D_CONV), F32))

    x = x_ref[...]
    h = _rms_norm(x, ng_ref[...]).astype(BF16)
    _store_blocked(uext_ref, POOL_PAD, _project(h, win_ref, 0, D_POOL))
    g_pool = _project(h, win_ref, D_POOL, 2 * D_POOL)
    v_a = _project(h, win_ref, 2 * D_POOL, 2 * D_POOL + D_CONV)
    v_b = _project(h, win_ref, 2 * D_POOL + D_CONV, 2 * D_POOL + 2 * D_CONV)
    _store_blocked(vext_ref, CONV_PAD, v_a * jax.nn.sigmoid(v_b))
    g_conv = _project(h, win_ref, 2 * D_POOL + 2 * D_CONV, D_IN)

    row = lax.broadcasted_iota(jnp.int32, (ROW_CHUNK, LANES), 0)

    def first_cnt(w):
        return jnp.where(j == 0, jnp.minimum(row + 1, w), w).astype(F32)

    for r0 in range(0, tt, ROW_CHUNK):
        pooled = _pool_rows(uext_ref, POOL_PAD + r0, ROW_CHUNK, first_cnt if r0 == 0 else None)
        pooled_ref[r0:r0 + ROW_CHUNK, :] = pooled.astype(BF16)
        c = _conv_rows(vext_ref, CONV_PAD + r0, ROW_CHUNK, dww_ref, dwb_ref)
        act_ref[r0:r0 + ROW_CHUNK, :] = _conv_post(c, lng_ref[...], lnb_ref[...]).astype(BF16)

    pool_out = _silu(g_pool) * (_pool_mix(pooled_ref, bd_ref) * ps_ref[...])
    cat_ref[:, 0:D_POOL] = pool_out.astype(BF16)
    cmix = jnp.dot(act_ref[...], pww_ref[...], preferred_element_type=F32) + pwb_ref[...]
    cat_ref[:, D_POOL:D_MODEL] = (_silu(g_conv) * cmix).astype(BF16)

    y = x + jnp.dot(cat_ref[...], wout_ref[...], preferred_element_type=F32)
    y_ref[...] = _rms_norm(y, fg_ref[...])

    @pl.when(j == pl.num_programs(1) - 1)
    def _():
        sp_ref[...] = _load_blocked(uext_ref, POOL_PAD + tt - POOL_HIST, POOL_HIST)
        sc_ref[...] = _load_blocked(vext_ref, CONV_PAD + tt - CONV_HIST, CONV_HIST)

    _store_blocked(uext_ref, 0, _load_blocked(uext_ref, tt, POOL_PAD))
    _store_blocked(vext_ref, 0, _load_blocked(vext_ref, tt, CONV_PAD))


def _sample_kernel(x_ref, cp_ref, cc_ref, ng_ref, win_ref, bd_ref, ps_ref, dww_ref, dwb_ref, lng_ref,
                   lnb_ref, pww_ref, pwb_ref, wout_ref, fg_ref,
                   y_ref, sp_ref, sc_ref,
                   uext_ref, vext_ref, pooled_ref, act_ref, cat_ref, *, n_streams, seq):
    x = x_ref[...]
    h = _rms_norm(x, ng_ref[...]).astype(BF16)
    u = _project(h, win_ref, 0, D_POOL)
    g_pool = _project(h, win_ref, D_POOL, 2 * D_POOL)
    v_a = _project(h, win_ref, 2 * D_POOL, 2 * D_POOL + D_CONV)
    v_b = _project(h, win_ref, 2 * D_POOL + D_CONV, 2 * D_POOL + 2 * D_CONV)
    v = v_a * jax.nn.sigmoid(v_b)
    g_conv = _project(h, win_ref, 2 * D_POOL + 2 * D_CONV, D_IN)

    for s in range(n_streams):
        ue = uext_ref.at[s]
        ve = vext_ref.at[s]
        _store_blocked(ue, POOL_PAD - POOL_HIST, cp_ref[s])
        _store_blocked(ue, POOL_PAD, u[s * seq:(s + 1) * seq, :])
        _store_blocked(ve, CONV_PAD - CONV_HIST, cc_ref[s])
        _store_blocked(ve, CONV_PAD, v[s * seq:(s + 1) * seq, :])
        pooled_ref[s * seq:(s + 1) * seq, :] = _pool_rows(ue, POOL_PAD, seq).astype(BF16)
        c = _conv_rows(ve, CONV_PAD, seq, dww_ref, dwb_ref)
        act_ref[s * seq:(s + 1) * seq, :] = _conv_post(c, lng_ref[...], lnb_ref[...]).astype(BF16)
        sp_ref[s] = _load_blocked(ue, POOL_PAD + seq - POOL_HIST, POOL_HIST)
        sc_ref[s] = _load_blocked(ve, CONV_PAD + seq - CONV_HIST, CONV_HIST)

    pool_out = _silu(g_pool) * (_pool_mix(pooled_ref, bd_ref) * ps_ref[...])
    cat_ref[:, 0:D_POOL] = pool_out.astype(BF16)
    cmix = jnp.dot(act_ref[...], pww_ref[...], preferred_element_type=F32) + pwb_ref[...]
    cat_ref[:, D_POOL:D_MODEL] = (_silu(g_conv) * cmix).astype(BF16)

    y = x + jnp.dot(cat_ref[...], wout_ref[...], preferred_element_type=F32)
    y_ref[...] = _rms_norm(y, fg_ref[...])


def _full(shape):
    return pl.BlockSpec(shape, lambda *_: (0,) * len(shape))


def _weight_specs():
    return [
        _full((1, D_MODEL)),
        _full((D_MODEL, D_IN)),
        _full((2, 2 * POOL_GROUP, 2 * POOL_GROUP)),
        _full((1, D_POOL)),
        _full((CONV_WIDTH, D_CONV)),
        _full((1, D_CONV)),
        _full((1, D_CONV)),
        _full((1, D_CONV)),
        _full((D_CONV, D_CONV)),
        _full((1, D_CONV)),
        _full((D_MODEL, D_MODEL)),
        _full((1, D_MODEL)),
    ]


def kernel(x_prompt, x_sample, cache_pool, cache_conv, norm_g, w_in, pool_mix, pool_scale, dw_w, dw_b,
           ln_g, ln_b, pw_w, pw_b, w_out, final_g):
    batch, seq, d_model = x_prompt.shape
    dec_batch, dec_seq, _ = x_sample.shape
    assert d_model == D_MODEL and w_in.shape == (1, D_MODEL, D_IN)
    assert seq % TIME_TILE == 0 and dec_seq % 8 == 0

    pm = pool_mix[0].astype(BF16)
    zero = jnp.zeros((POOL_GROUP, POOL_GROUP), BF16)
    bd = jnp.stack([jnp.block([[pm[2 * i], zero], [zero, pm[2 * i + 1]]]) for i in range(2)])
    weights = (norm_g, w_in[0].astype(BF16), bd, pool_scale, dw_w[0], dw_b, ln_g, ln_b,
               pw_w[0].astype(BF16), pw_b, w_out[0].astype(BF16), final_g.reshape(1, D_MODEL))

    n_tiles = seq // TIME_TILE
    y_prompt, sp_prompt, sc_prompt = pl.pallas_call(
        _prompt_kernel,
        grid=(batch, n_tiles),
        in_specs=[pl.BlockSpec((None, TIME_TILE, D_MODEL), lambda b, j: (b, j, 0))] + _weight_specs(),
        out_specs=[
            pl.BlockSpec((None, TIME_TILE, D_MODEL), lambda b, j: (b, j, 0)),
            pl.BlockSpec((None, None, POOL_HIST, D_POOL), lambda b, j: (0, b, 0, 0)),
            pl.BlockSpec((None, None, CONV_HIST, D_CONV), lambda b, j: (0, b, 0, 0)),
        ],
        out_shape=[
            jax.ShapeDtypeStruct((batch, seq, D_MODEL), F32),
            jax.ShapeDtypeStruct((1, batch, POOL_HIST, D_POOL), F32),
            jax.ShapeDtypeStruct((1, batch, CONV_HIST, D_CONV), F32),
        ],
        scratch_shapes=[
            pltpu.VMEM((D_POOL // LANES, POOL_PAD + TIME_TILE, LANES), F32),
            pltpu.VMEM((D_CONV // LANES, CONV_PAD + TIME_TILE, LANES), F32),
            pltpu.VMEM((TIME_TILE, D_POOL), BF16),
            pltpu.VMEM((TIME_TILE, D_CONV), BF16),
            pltpu.VMEM((TIME_TILE, D_MODEL), BF16),
        ],
        compiler_params=pltpu.CompilerParams(
            dimension_semantics=("arbitrary", "arbitrary"), vmem_limit_bytes=VMEM_LIMIT_BYTES),
        name="prompt_mixer",
    )(x_prompt, *weights)

    rows = dec_batch * dec_seq
    y_sample, sp_sample, sc_sample = pl.pallas_call(
        functools.partial(_sample_kernel, n_streams=dec_batch, seq=dec_seq),
        grid=(1,),
        in_specs=[_full((rows, D_MODEL)), _full((dec_batch, POOL_HIST, D_POOL)),
                  _full((dec_batch, CONV_HIST, D_CONV))] + _weight_specs(),
        out_specs=[_full((rows, D_MODEL)), _full((dec_batch, POOL_HIST, D_POOL)),
                   _full((dec_batch, CONV_HIST, D_CONV))],
        out_shape=[
            jax.ShapeDtypeStruct((rows, D_MODEL), F32),
            jax.ShapeDtypeStruct((dec_batch, POOL_HIST, D_POOL), F32),
            jax.ShapeDtypeStruct((dec_batch, CONV_HIST, D_CONV), F32),
        ],
        scratch_shapes=[
            pltpu.VMEM((dec_batch, D_POOL // LANES, POOL_PAD + dec_seq, LANES), F32),
            pltpu.VMEM((dec_batch, D_CONV // LANES, CONV_PAD + dec_seq, LANES), F32),
            pltpu.VMEM((rows, D_POOL), BF16),
            pltpu.VMEM((rows, D_CONV), BF16),
            pltpu.VMEM((rows, D_MODEL), BF16),
        ],
        compiler_params=pltpu.CompilerParams(
            dimension_semantics=("arbitrary",), vmem_limit_bytes=VMEM_LIMIT_BYTES),
        name="sample_mixer",
    )(x_sample.reshape(rows, D_MODEL), cache_pool[0], cache_conv[0], *weights)

    return (y_prompt, y_sample.reshape(dec_batch, dec_seq, D_MODEL), sp_prompt, sc_prompt,
            sp_sample[None], sc_sample[None])
```

```python
import functools

import jax
import jax.numpy as jnp
from jax import lax
from jax.experimental import pallas as pl
from jax.experimental.pallas import tpu as pltpu

D_MODEL = 1024
D_POOL = 512
D_CONV = 512
POOL_WINDOWS = (2, 4, 8, 16)
POOL_GROUP = 128
POOL_HIST = 15
CONV_WIDTH = 31
CONV_HIST = 30
D_IN = 2 * D_POOL + 3 * D_CONV
RMS_EPS = 1e-6
LN_EPS = 1e-5

LANES = 128
SUBLANES = 8
POOL_PAD = 16
CONV_PAD = 32
TIME_TILE = 512
ROW_CHUNK = 64
VMEM_LIMIT_BYTES = 56 * 1024 * 1024

F32 = jnp.float32
BF16 = jnp.bfloat16


def _rms_norm(x, g):
    ms = jnp.mean(x * x, axis=-1, keepdims=True)
    return x * lax.rsqrt(ms + RMS_EPS) * g


def _silu(x):
    return x * jax.nn.sigmoid(x)


def _store_blocked(ext, row0, val):
    for cb in range(val.shape[-1] // LANES):
        ext[cb, pl.ds(row0, val.shape[0]), :] = val[:, cb * LANES:(cb + 1) * LANES]


def _load_blocked(ext, row0, rows):
    return jnp.concatenate([ext[cb, pl.ds(row0, rows), :] for cb in range(ext.shape[0])], axis=-1)


def _pool_rows(uext, row0, rows, cnt=None):
    outs = []
    for gi, w in enumerate(POOL_WINDOWS):
        cur = uext[gi, pl.ds(row0, rows), :]
        span = min(w, SUBLANES)
        acc = cur
        for i in range(1, span):
            acc = acc + uext[gi, pl.ds(row0 - i, rows), :]
        if w > span:
            assert w == 2 * span
            head = uext[gi, pl.ds(row0 - span, span), :]
            for i in range(1, span):
                head = head + uext[gi, pl.ds(row0 - span - i, span), :]
            acc = acc + jnp.concatenate([head, acc[:rows - span]], axis=0)
        if cnt is None:
            outs.append(acc * (1.0 / w) - cur)
        else:
            outs.append(acc / cnt(w) - cur)
    return jnp.concatenate(outs, axis=-1)


def _pack_rows(vext, pk, row0, rows):
    for cb in range(vext.shape[0]):
        for odd in (0, 1):
            x = vext[cb, pl.ds(row0 + odd, rows), :].astype(BF16)
            pk[odd, cb, pl.ds(row0 // 2, rows // 2), :] = pltpu.bitcast(x, jnp.uint32)


def _conv_rows(pk, row0, rows, dwu_ref, dwb_ref):
    assert CONV_HIST % 2 == 0 and row0 % 2 == 0 and rows % 2 == 0
    outs = []
    for cb in range(D_CONV // LANES):
        sl = slice(cb * LANES, (cb + 1) * LANES)
        part = [None] * 4
        for k in range(CONV_WIDTH):
            odd = k % 2
            start = (row0 + k - CONV_HIST - odd) // 2
            x = pltpu.bitcast(pk[odd, cb, pl.ds(start, rows // 2), :], BF16)
            w = pltpu.bitcast(jnp.broadcast_to(dwu_ref[k:k + 1, sl], (rows // 2, LANES)), BF16)
            part[k % 4] = x * w if part[k % 4] is None else part[k % 4] + x * w
        acc = (part[0] + part[1]) + (part[2] + part[3])
        outs.append(acc.astype(F32) + dwb_ref[:, sl])
    return jnp.concatenate(outs, axis=-1)


def _conv_post(c, lng, lnb):
    mu = jnp.mean(c, axis=-1, keepdims=True)
    xc = c - mu
    var = jnp.mean(xc * xc, axis=-1, keepdims=True)
    return _silu(xc * lax.rsqrt(var + LN_EPS) * lng + lnb)


def _dot(a, b):
    return jnp.dot(a, b, preferred_element_type=F32)


def _mix_and_project(x, g_pool, g_conv, pooled_ref, act_ref, cat_ref, bd_ref, ps_ref, pww_ref, pwb_ref,
                     wout_ref, fg_ref):
    half = 2 * POOL_GROUP
    mix = jnp.concatenate([_dot(pooled_ref[:, i * half:(i + 1) * half], bd_ref[i])
                           for i in range(D_POOL // half)], axis=-1)
    cat_ref[:, 0:D_POOL] = (_silu(g_pool) * (mix * ps_ref[...])).astype(BF16)
    cmix = _dot(act_ref[...], pww_ref[...]) + pwb_ref[...]
    cat_ref[:, D_POOL:D_MODEL] = (_silu(g_conv) * cmix).astype(BF16)
    return _rms_norm(x + _dot(cat_ref[...], wout_ref[...]), fg_ref[...])


def _prompt_kernel(x_ref, ng_ref, win_ref, bd_ref, ps_ref, dwu_ref, dwb_ref, lng_ref, lnb_ref,
                   pww_ref, pwb_ref, wout_ref, fg_ref,
                   y_ref, sp_ref, sc_ref,
                   uext_ref, vext_ref, pk_ref, pooled_ref, act_ref, cat_ref):
    j = pl.program_id(1)
    tt = TIME_TILE

    @pl.when(j == 0)
    def _():
        _store_blocked(uext_ref, 0, jnp.zeros((POOL_PAD, D_POOL), F32))
        _store_blocked(vext_ref, 0, jnp.zeros((CONV_PAD, D_CONV), F32))
        _store_blocked(vext_ref, CONV_PAD + tt, jnp.zeros((SUBLANES, D_CONV), F32))

    x = x_ref[...]
    h = _rms_norm(x, ng_ref[...]).astype(BF16)

    def proj(lo, hi):
        return _dot(h, win_ref[:, lo:hi])

    _store_blocked(uext_ref, POOL_PAD, proj(0, D_POOL))
    g_pool = proj(D_POOL, 2 * D_POOL)
    v_a = proj(2 * D_POOL, 2 * D_POOL + D_CONV)
    v_b = proj(2 * D_POOL + D_CONV, 2 * D_POOL + 2 * D_CONV)
    _store_blocked(vext_ref, CONV_PAD, v_a * jax.nn.sigmoid(v_b))
    g_conv = proj(2 * D_POOL + 2 * D_CONV, D_IN)

    row = lax.broadcasted_iota(jnp.int32, (ROW_CHUNK, LANES), 0)

    def first_cnt(w):
        return jnp.where(j == 0, jnp.minimum(row + 1, w), w).astype(F32)

    for r0 in range(0, tt, ROW_CHUNK):
        pooled = _pool_rows(uext_ref, POOL_PAD + r0, ROW_CHUNK, first_cnt if r0 == 0 else None)
        pooled_ref[r0:r0 + ROW_CHUNK, :] = pooled.astype(BF16)
        lo = 0 if r0 == 0 else CONV_PAD + r0
        _pack_rows(vext_ref, pk_ref, lo, CONV_PAD + r0 + ROW_CHUNK - lo)
        c = _conv_rows(pk_ref, CONV_PAD + r0, ROW_CHUNK, dwu_ref, dwb_ref)
        act_ref[r0:r0 + ROW_CHUNK, :] = _conv_post(c, lng_ref[...], lnb_ref[...]).astype(BF16)

    y_ref[...] = _mix_and_project(x, g_pool, g_conv, pooled_ref, act_ref, cat_ref, bd_ref, ps_ref, pww_ref,
                                  pwb_ref, wout_ref, fg_ref)

    @pl.when(j == pl.num_programs(1) - 1)
    def _():
        sp_ref[...] = _load_blocked(uext_ref, POOL_PAD + tt - POOL_HIST, POOL_HIST)
        sc_ref[...] = _load_blocked(vext_ref, CONV_PAD + tt - CONV_HIST, CONV_HIST)

    _store_blocked(uext_ref, 0, _load_blocked(uext_ref, tt, POOL_PAD))
    _store_blocked(vext_ref, 0, _load_blocked(vext_ref, tt, CONV_PAD))


def _sample_kernel(x_ref, cp_ref, cc_ref, ng_ref, win_ref, bd_ref, ps_ref, dwu_ref, dwb_ref, lng_ref,
                   lnb_ref, pww_ref, pwb_ref, wout_ref, fg_ref,
                   y_ref, sp_ref, sc_ref,
                   uext_ref, vext_ref, pk_ref, pooled_ref, act_ref, cat_ref, *, n_streams, seq):
    x = x_ref[...]
    h = _rms_norm(x, ng_ref[...]).astype(BF16)

    def proj(lo, hi):
        return _dot(h, win_ref[:, lo:hi])

    u = proj(0, D_POOL)
    g_pool = proj(D_POOL, 2 * D_POOL)
    v = proj(2 * D_POOL, 2 * D_POOL + D_CONV) * jax.nn.sigmoid(proj(2 * D_POOL + D_CONV, 2 * D_POOL + 2 * D_CONV))
    g_conv = proj(2 * D_POOL + 2 * D_CONV, D_IN)

    for s in range(n_streams):
        ue = uext_ref.at[s]
        ve = vext_ref.at[s]
        _store_blocked(ue, POOL_PAD - POOL_HIST, cp_ref[s])
        _store_blocked(ue, POOL_PAD, u[s * seq:(s + 1) * seq, :])
        _store_blocked(ve, CONV_PAD - CONV_HIST, cc_ref[s])
        _store_blocked(ve, CONV_PAD, v[s * seq:(s + 1) * seq, :])
        pooled_ref[s * seq:(s + 1) * seq, :] = _pool_rows(ue, POOL_PAD, seq).astype(BF16)
        _store_blocked(ve, CONV_PAD + seq, jnp.zeros((SUBLANES, D_CONV), F32))
        first = CONV_PAD - CONV_HIST
        _pack_rows(ve, pk_ref, first, CONV_HIST + seq)
        c = _conv_rows(pk_ref, CONV_PAD, seq, dwu_ref, dwb_ref)
        act_ref[s * seq:(s + 1) * seq, :] = _conv_post(c, lng_ref[...], lnb_ref[...]).astype(BF16)
        sp_ref[s] = _load_blocked(ue, POOL_PAD + seq - POOL_HIST, POOL_HIST)
        sc_ref[s] = _load_blocked(ve, CONV_PAD + seq - CONV_HIST, CONV_HIST)

    y_ref[...] = _mix_and_project(x, g_pool, g_conv, pooled_ref, act_ref, cat_ref, bd_ref, ps_ref, pww_ref,
                                  pwb_ref, wout_ref, fg_ref)


def _full(shape):
    return pl.BlockSpec(shape, lambda *_: (0,) * len(shape))


def _weight_specs():
    return [
        _full((1, D_MODEL)),
        _full((D_MODEL, D_IN)),
        _full((2, 2 * POOL_GROUP, 2 * POOL_GROUP)),
        _full((1, D_POOL)),
        _full((CONV_WIDTH, D_CONV)),
        _full((1, D_CONV)),
        _full((1, D_CONV)),
        _full((1, D_CONV)),
        _full((D_CONV, D_CONV)),
        _full((1, D_CONV)),
        _full((D_MODEL, D_MODEL)),
        _full((1, D_MODEL)),
    ]


def kernel(x_prompt, x_sample, cache_pool, cache_conv, norm_g, w_in, pool_mix, pool_scale, dw_w, dw_b,
           ln_g, ln_b, pw_w, pw_b, w_out, final_g):
    batch, seq, d_model = x_prompt.shape
    dec_batch, dec_seq, _ = x_sample.shape
    assert d_model == D_MODEL and w_in.shape == (1, D_MODEL, D_IN)
    assert seq % TIME_TILE == 0 and dec_seq % 8 == 0

    pm = pool_mix[0].astype(BF16)
    zero = jnp.zeros((POOL_GROUP, POOL_GROUP), BF16)
    bd = jnp.stack([jnp.block([[pm[2 * i], zero], [zero, pm[2 * i + 1]]]) for i in range(2)])
    dw_bf = dw_w[0].astype(BF16)
    dwu = lax.bitcast_convert_type(jnp.stack([dw_bf, dw_bf], axis=-1), jnp.uint32)
    weights = (norm_g, w_in[0].astype(BF16), bd, pool_scale, dwu, dw_b, ln_g, ln_b,
               pw_w[0].astype(BF16), pw_b, w_out[0].astype(BF16), final_g.reshape(1, D_MODEL))

    n_tiles = seq // TIME_TILE
    y_prompt, sp_prompt, sc_prompt = pl.pallas_call(
        _prompt_kernel,
        grid=(batch, n_tiles),
        in_specs=[pl.BlockSpec((None, TIME_TILE, D_MODEL), lambda b, j: (b, j, 0))] + _weight_specs(),
        out_specs=[
            pl.BlockSpec((None, TIME_TILE, D_MODEL), lambda b, j: (b, j, 0)),
            pl.BlockSpec((None, None, POOL_HIST, D_POOL), lambda b, j: (0, b, 0, 0)),
            pl.BlockSpec((None, None, CONV_HIST, D_CONV), lambda b, j: (0, b, 0, 0)),
        ],
        out_shape=[
            jax.ShapeDtypeStruct((batch, seq, D_MODEL), F32),
            jax.ShapeDtypeStruct((1, batch, POOL_HIST, D_POOL), F32),
            jax.ShapeDtypeStruct((1, batch, CONV_HIST, D_CONV), F32),
        ],
        scratch_shapes=[
            pltpu.VMEM((D_POOL // LANES, POOL_PAD + TIME_TILE, LANES), F32),
            pltpu.VMEM((D_CONV // LANES, CONV_PAD + TIME_TILE + SUBLANES, LANES), F32),
            pltpu.VMEM((2, D_CONV // LANES, (CONV_PAD + TIME_TILE) // 2, LANES), jnp.uint32),
            pltpu.VMEM((TIME_TILE, D_POOL), BF16),
            pltpu.VMEM((TIME_TILE, D_CONV), BF16),
            pltpu.VMEM((TIME_TILE, D_MODEL), BF16),
        ],
        compiler_params=pltpu.CompilerParams(
            dimension_semantics=("arbitrary", "arbitrary"), vmem_limit_bytes=VMEM_LIMIT_BYTES),
        name="prompt_mixer",
    )(x_prompt, *weights)

    rows = dec_batch * dec_seq
    y_sample, sp_sample, sc_sample = pl.pallas_call(
        functools.partial(_sample_kernel, n_streams=dec_batch, seq=dec_seq),
        grid=(1,),
        in_specs=[_full((rows, D_MODEL)), _full((dec_batch, POOL_HIST, D_POOL)),
                  _full((dec_batch, CONV_HIST, D_CONV))] + _weight_specs(),
        out_specs=[_full((rows, D_MODEL)), _full((dec_batch, POOL_HIST, D_POOL)),
                   _full((dec_batch, CONV_HIST, D_CONV))],
        out_shape=[
            jax.ShapeDtypeStruct((rows, D_MODEL), F32),
            jax.ShapeDtypeStruct((dec_batch, POOL_HIST, D_POOL), F32),
            jax.ShapeDtypeStruct((dec_batch, CONV_HIST, D_CONV), F32),
        ],
        scratch_shapes=[
            pltpu.VMEM((dec_batch, D_POOL // LANES, POOL_PAD + dec_seq, LANES), F32),
            pltpu.VMEM((dec_batch, D_CONV // LANES, CONV_PAD + dec_seq + SUBLANES, LANES), F32),
            pltpu.VMEM((2, D_CONV // LANES, (CONV_PAD + dec_seq) // 2, LANES), jnp.uint32),
            pltpu.VMEM((rows, D_POOL), BF16),
            pltpu.VMEM((rows, D_CONV), BF16),
            pltpu.VMEM((rows, D_MODEL), BF16),
        ],
        compiler_params=pltpu.CompilerParams(
            dimension_semantics=("arbitrary",), vmem_limit_bytes=VMEM_LIMIT_BYTES),
        name="sample_mixer",
    )(x_sample.reshape(rows, D_MODEL), cache_pool[0], cache_conv[0], *weights)

    return (y_prompt, y_sample.reshape(dec_batch, dec_seq, D_MODEL), sp_prompt, sc_prompt,
            sp_sample[None], sc_sample[None])
```

```python
import functools

import jax
import jax.numpy as jnp
from jax import lax
from jax.experimental import pallas as pl
from jax.experimental.pallas import tpu as pltpu

D_MODEL = 1024
D_POOL = 512
D_CONV = 512
POOL_WINDOWS = (2, 4, 8, 16)
POOL_GROUP = 128
POOL_HIST = 15
CONV_WIDTH = 31
CONV_HIST = 30
D_IN = 2 * D_POOL + 3 * D_CONV
RMS_EPS = 1e-6
LN_EPS = 1e-5

LANES = 128
SUBLANES = 8
BF16_ROWS = 2 * SUBLANES
POOL_PAD = 16
CONV_PAD = 32
TIME_TILE = 512
ROW_CHUNK = 64
VMEM_LIMIT_BYTES = 56 * 1024 * 1024

F32 = jnp.float32
BF16 = jnp.bfloat16
U32 = jnp.uint32


def _rms_norm(x, g):
    ms = jnp.mean(x * x, axis=-1, keepdims=True)
    return x * lax.rsqrt(ms + RMS_EPS) * g


def _silu(x):
    return x * jax.nn.sigmoid(x)


def _store_blocked(ext, row0, val):
    for cb in range(val.shape[-1] // LANES):
        ext[cb, pl.ds(row0, val.shape[0]), :] = val[:, cb * LANES:(cb + 1) * LANES]


def _load_blocked(ext, row0, rows):
    return jnp.concatenate([ext[cb, pl.ds(row0, rows), :] for cb in range(ext.shape[0])], axis=-1)


def _pool_rows(uext, row0, rows, cnt=None):
    outs = []
    for gi, w in enumerate(POOL_WINDOWS):
        cur = uext[gi, pl.ds(row0, rows), :]
        span = min(w, SUBLANES)
        acc = cur
        for i in range(1, span):
            acc = acc + uext[gi, pl.ds(row0 - i, rows), :]
        if w > span:
            assert w == 2 * span
            head = uext[gi, pl.ds(row0 - span, span), :]
            for i in range(1, span):
                head = head + uext[gi, pl.ds(row0 - span - i, span), :]
            acc = acc + jnp.concatenate([head, acc[:rows - span]], axis=0)
        if cnt is None:
            outs.append(acc * (1.0 / w) - cur)
        else:
            outs.append(acc / cnt(w) - cur)
    return jnp.concatenate(outs, axis=-1)


def _pack_rows(vext, pk, row0, rows):
    for cb in range(vext.shape[0]):
        for odd in (0, 1):
            x = vext[cb, pl.ds(row0 + odd, rows), :].astype(BF16)
            pk[odd, cb, pl.ds(row0 // 2, rows // 2), :] = pltpu.bitcast(x, U32)


def _conv_rows(pk, row0, rows, dwu_ref, dwb_ref):
    assert CONV_HIST % 2 == 0 and row0 % 2 == 0 and rows % BF16_ROWS == 0
    outs = []
    for cb in range(D_CONV // LANES):
        sl = slice(cb * LANES, (cb + 1) * LANES)
        part = [None] * 4
        for k in range(CONV_WIDTH):
            odd = k % 2
            start = (row0 + k - CONV_HIST - odd) // 2
            x = pltpu.bitcast(pk[odd, cb, pl.ds(start, rows // 2), :], BF16)
            w = pltpu.bitcast(jnp.broadcast_to(dwu_ref[k:k + 1, sl], (rows // 2, LANES)), BF16)
            part[k % 4] = x * w if part[k % 4] is None else part[k % 4] + x * w
        acc = (part[0] + part[1]) + (part[2] + part[3])
        outs.append(acc.astype(F32) + dwb_ref[:, sl])
    return jnp.concatenate(outs, axis=-1)


def _conv_post(c, lng, lnb):
    mu = jnp.mean(c, axis=-1, keepdims=True)
    xc = c - mu
    var = jnp.mean(xc * xc, axis=-1, keepdims=True)
    return _silu(xc * lax.rsqrt(var + LN_EPS) * lng + lnb)


def _dot(a, w):
    return lax.dot_general(a, w, (((1,), (0,)), ((), ())), preferred_element_type=F32)


def _prepare_weights(pmix_ref, dww_ref, bd_ref, dwu_ref):
    half = 2 * POOL_GROUP
    bd_ref[...] = jnp.zeros(bd_ref.shape, bd_ref.dtype)
    for g in range(len(POOL_WINDOWS)):
        lo = (g % 2) * POOL_GROUP
        bd_ref[g // 2, lo:lo + POOL_GROUP, lo:lo + POOL_GROUP] = pmix_ref[g].astype(BF16)
    assert bd_ref.shape == (D_POOL // half, half, half)
    for k in range(CONV_WIDTH):
        tile = jnp.broadcast_to(dww_ref[k:k + 1, :], (BF16_ROWS, D_CONV)).astype(BF16)
        dwu_ref[k:k + 1, :] = pltpu.bitcast(tile, U32)[0:1, :]


def _mix_and_project(x, g_pool, g_conv, pooled_ref, act_ref, cat_ref, bd_ref, ps_ref, pww_ref, pwb_ref,
                     wout_ref, fg_ref):
    half = 2 * POOL_GROUP
    mix = jnp.concatenate([_dot(pooled_ref[:, i * half:(i + 1) * half], bd_ref[i])
                           for i in range(D_POOL // half)], axis=-1)
    cat_ref[:, 0:D_POOL] = (_silu(g_pool) * (mix * ps_ref[...])).astype(BF16)
    cmix = _dot(act_ref[...], pww_ref[...]) + pwb_ref[...]
    cat_ref[:, D_POOL:D_MODEL] = (_silu(g_conv) * cmix).astype(BF16)
    return _rms_norm(x + _dot(cat_ref[...], wout_ref[...]), fg_ref[...])


def _project(x, ng_ref, win_ref):
    h = _rms_norm(x, ng_ref[...]).astype(BF16)

    def proj(lo, hi):
        return _dot(h, win_ref[:, lo:hi])

    u = proj(0, D_POOL)
    g_pool = proj(D_POOL, 2 * D_POOL)
    v = proj(2 * D_POOL, 2 * D_POOL + D_CONV) * jax.nn.sigmoid(proj(2 * D_POOL + D_CONV, 2 * D_POOL + 2 * D_CONV))
    g_conv = proj(2 * D_POOL + 2 * D_CONV, D_IN)
    return u, g_pool, v, g_conv


def _prompt_tile(b, j, last_j, n_streams, x_ref, w, y_ref, sp_ref, sc_ref,
                 uext_ref, vext_ref, pk_ref, pooled_ref, act_ref, cat_ref):
    tt = TIME_TILE

    @pl.when(j == 0)
    def _():
        _store_blocked(uext_ref, 0, jnp.zeros((POOL_PAD, D_POOL), F32))
        _store_blocked(vext_ref, 0, jnp.zeros((CONV_PAD, D_CONV), F32))
        _store_blocked(vext_ref, CONV_PAD + tt, jnp.zeros((SUBLANES, D_CONV), F32))

    x = x_ref[...]
    u, g_pool, v, g_conv = _project(x, w.ng, w.win)
    _store_blocked(uext_ref, POOL_PAD, u)
    _store_blocked(vext_ref, CONV_PAD, v)

    row = lax.broadcasted_iota(jnp.int32, (ROW_CHUNK, LANES), 0)

    def first_cnt(win):
        return jnp.where(j == 0, jnp.minimum(row + 1, win), win).astype(F32)

    for r0 in range(0, tt, ROW_CHUNK):
        pooled = _pool_rows(uext_ref, POOL_PAD + r0, ROW_CHUNK, first_cnt if r0 == 0 else None)
        pooled_ref[r0:r0 + ROW_CHUNK, :] = pooled.astype(BF16)
        lo = 0 if r0 == 0 else CONV_PAD + r0
        _pack_rows(vext_ref, pk_ref, lo, CONV_PAD + r0 + ROW_CHUNK - lo)
        c = _conv_rows(pk_ref, CONV_PAD + r0, ROW_CHUNK, w.dwu, w.dwb)
        act_ref[r0:r0 + ROW_CHUNK, :] = _conv_post(c, w.lng[...], w.lnb[...]).astype(BF16)

    y_ref[...] = _mix_and_project(x, g_pool, g_conv, pooled_ref, act_ref, cat_ref, w.bd, w.ps, w.pww, w.pwb,
                                  w.wout, w.fg)

    for bb in range(n_streams):
        @pl.when((j == last_j) & (b == bb))
        def _(bb=bb):
            sp_ref[:, bb, :] = _load_blocked(uext_ref, POOL_PAD + tt - POOL_HIST, POOL_HIST)
            sc_ref[:, bb, :] = _load_blocked(vext_ref, CONV_PAD + tt - CONV_HIST, CONV_HIST)

    _store_blocked(uext_ref, 0, _load_blocked(uext_ref, tt, POOL_PAD))
    _store_blocked(vext_ref, 0, _load_blocked(vext_ref, tt, CONV_PAD))


def _sample_rows(n_streams, seq, x_ref, cp_ref, cc_ref, w, y_ref, sp_ref, sc_ref,
                 uext_ref, vext_ref, pk_ref, pooled_ref, act_ref, cat_ref):
    x = x_ref[...]
    u, g_pool, v, g_conv = _project(x, w.ng, w.win)

    for s in range(n_streams):
        rows = slice(s * seq, (s + 1) * seq)
        _store_blocked(uext_ref, POOL_PAD - POOL_HIST, cp_ref[:, s, :])
        _store_blocked(uext_ref, POOL_PAD, u[rows, :])
        _store_blocked(vext_ref, CONV_PAD - CONV_HIST, cc_ref[:, s, :])
        _store_blocked(vext_ref, CONV_PAD, v[rows, :])
        pooled_ref[rows, :] = _pool_rows(uext_ref, POOL_PAD, seq).astype(BF16)
        _store_blocked(vext_ref, CONV_PAD + seq, jnp.zeros((SUBLANES, D_CONV), F32))
        _pack_rows(vext_ref, pk_ref, CONV_PAD - CONV_HIST, CONV_HIST + seq)
        c = _conv_rows(pk_ref, CONV_PAD, seq, w.dwu, w.dwb)
        act_ref[rows, :] = _conv_post(c, w.lng[...], w.lnb[...]).astype(BF16)
        sp_ref[:, s, :] = _load_blocked(uext_ref, POOL_PAD + seq - POOL_HIST, POOL_HIST)
        sc_ref[:, s, :] = _load_blocked(vext_ref, CONV_PAD + seq - CONV_HIST, CONV_HIST)

    y_ref[...] = _mix_and_project(x, g_pool, g_conv, pooled_ref.at[0:n_streams * seq], act_ref.at[0:n_streams * seq],
                                  cat_ref.at[0:n_streams * seq], w.bd, w.ps, w.pww, w.pwb, w.wout, w.fg)


class _Weights:
    def __init__(self, ng, win, ps, dwb, lng, lnb, pww, pwb, wout, fg, bd, dwu):
        self.ng, self.win, self.ps, self.dwb, self.lng, self.lnb = ng, win, ps, dwb, lng, lnb
        self.pww, self.pwb, self.wout, self.fg, self.bd, self.dwu = pww, pwb, wout, fg, bd, dwu


def _mixer_kernel(xp_ref, xs_ref, cp_ref, cc_ref, ng_ref, win_ref, pmix_ref, ps_ref, dww_ref, dwb_ref,
                  lng_ref, lnb_ref, pww_ref, pwb_ref, wout_ref, fg_ref,
                  yp_ref, ys_ref, spp_ref, scp_ref, sps_ref, scs_ref,
                  bd_ref, dwu_ref, uext_ref, vext_ref, pk_ref, pooled_ref, act_ref, cat_ref,
                  *, n_tiles, tiles_per_stream, n_prompt_streams, n_sample_streams, sample_seq):
    s = pl.program_id(0)
    w = _Weights(ng_ref, win_ref, ps_ref, dwb_ref, lng_ref, lnb_ref, pww_ref, pwb_ref, wout_ref, fg_ref,
                 bd_ref, dwu_ref)

    @pl.when(s == 0)
    def _():
        _prepare_weights(pmix_ref, dww_ref, bd_ref, dwu_ref)

    @pl.when(s < n_tiles)
    def _():
        _prompt_tile(s // tiles_per_stream, lax.rem(s, tiles_per_stream), tiles_per_stream - 1,
                     n_prompt_streams, xp_ref, w, yp_ref, spp_ref, scp_ref,
                     uext_ref, vext_ref, pk_ref, pooled_ref, act_ref, cat_ref)

    @pl.when(s == n_tiles)
    def _():
        _sample_rows(n_sample_streams, sample_seq, xs_ref, cp_ref, cc_ref, w, ys_ref, sps_ref, scs_ref,
                     uext_ref, vext_ref, pk_ref, pooled_ref, act_ref, cat_ref)


def _resident(shape):
    return pl.BlockSpec(shape, lambda s: (0,) * len(shape), pipeline_mode=pl.Buffered(1))


def _whole(shape):
    return pl.BlockSpec(shape, lambda s: (0,) * len(shape))


def kernel(x_prompt, x_sample, cache_pool, cache_conv, norm_g, w_in, pool_mix, pool_scale, dw_w, dw_b,
           ln_g, ln_b, pw_w, pw_b, w_out, final_g):
    batch, seq, d_model = x_prompt.shape
    dec_batch, dec_seq, _ = x_sample.shape
    assert d_model == D_MODEL and w_in.shape == (1, D_MODEL, D_IN)
    assert seq % TIME_TILE == 0 and dec_seq % BF16_ROWS == 0 and dec_batch * dec_seq <= TIME_TILE
    tiles_per_stream = seq // TIME_TILE
    n_tiles = batch * tiles_per_stream
    rows = dec_batch * dec_seq

    def tile_index(s):
        t = jnp.minimum(s, n_tiles - 1)
        return (t // tiles_per_stream, t % tiles_per_stream, 0)

    y_prompt, y_sample, sp_prompt, sc_prompt, sp_sample, sc_sample = pl.pallas_call(
        functools.partial(_mixer_kernel, n_tiles=n_tiles, tiles_per_stream=tiles_per_stream,
                          n_prompt_streams=batch, n_sample_streams=dec_batch, sample_seq=dec_seq),
        grid=(n_tiles + 1,),
        in_specs=[
            pl.BlockSpec((None, TIME_TILE, D_MODEL), tile_index),
            _whole((rows, D_MODEL)),
            _whole((POOL_HIST, dec_batch, D_POOL)),
            _whole((CONV_HIST, dec_batch, D_CONV)),
            _resident((1, D_MODEL)),
            _resident((D_MODEL, D_IN)),
            _resident((len(POOL_WINDOWS), POOL_GROUP, POOL_GROUP)),
            _resident((1, D_POOL)),
            _resident((CONV_WIDTH, D_CONV)),
            _resident((1, D_CONV)),
            _resident((1, D_CONV)),
            _resident((1, D_CONV)),
            _resident((D_CONV, D_CONV)),
            _resident((1, D_CONV)),
            _resident((D_MODEL, D_MODEL)),
            _resident((1, D_MODEL)),
        ],
        out_specs=[
            pl.BlockSpec((None, TIME_TILE, D_MODEL), tile_index),
            _whole((rows, D_MODEL)),
            _whole((POOL_HIST, batch, D_POOL)),
            _whole((CONV_HIST, batch, D_CONV)),
            _whole((POOL_HIST, dec_batch, D_POOL)),
            _whole((CONV_HIST, dec_batch, D_CONV)),
        ],
        out_shape=[
            jax.ShapeDtypeStruct((batch, seq, D_MODEL), F32),
            jax.ShapeDtypeStruct((rows, D_MODEL), F32),
            jax.ShapeDtypeStruct((POOL_HIST, batch, D_POOL), F32),
            jax.ShapeDtypeStruct((CONV_HIST, batch, D_CONV), F32),
            jax.ShapeDtypeStruct((POOL_HIST, dec_batch, D_POOL), F32),
            jax.ShapeDtypeStruct((CONV_HIST, dec_batch, D_CONV), F32),
        ],
        scratch_shapes=[
            pltpu.VMEM((D_POOL // (2 * POOL_GROUP), 2 * POOL_GROUP, 2 * POOL_GROUP), BF16),
            pltpu.VMEM((CONV_WIDTH, D_CONV), U32),
            pltpu.VMEM((D_POOL // LANES, POOL_PAD + TIME_TILE, LANES), F32),
            pltpu.VMEM((D_CONV // LANES, CONV_PAD + TIME_TILE + SUBLANES, LANES), F32),
            pltpu.VMEM((2, D_CONV // LANES, (CONV_PAD + TIME_TILE) // 2, LANES), U32),
            pltpu.VMEM((TIME_TILE, D_POOL), BF16),
            pltpu.VMEM((TIME_TILE, D_CONV), BF16),
            pltpu.VMEM((TIME_TILE, D_MODEL), BF16),
        ],
        compiler_params=pltpu.CompilerParams(
            dimension_semantics=("arbitrary",), vmem_limit_bytes=VMEM_LIMIT_BYTES),
        name="stream_mixer",
    )(x_prompt, x_sample.reshape(rows, D_MODEL), jnp.swapaxes(cache_pool[0], 0, 1),
      jnp.swapaxes(cache_conv[0], 0, 1), norm_g, w_in[0], pool_mix[0], pool_scale, dw_w[0], dw_b, ln_g, ln_b,
      pw_w[0], pw_b, w_out[0], final_g.reshape(1, D_MODEL))

    def as_state(t):
        return jnp.swapaxes(t, 0, 1)[None]

    return (y_prompt, y_sample.reshape(dec_batch, dec_seq, D_MODEL), as_state(sp_prompt), as_state(sc_prompt),
            as_state(sp_sample), as_state(sc_sample))
```

```python
import functools

import jax
import jax.numpy as jnp
from jax import lax
from jax.experimental import pallas as pl
from jax.experimental.pallas import tpu as pltpu

D_MODEL = 1024
D_POOL = 512
D_CONV = 512
POOL_WINDOWS = (2, 4, 8, 16)
POOL_GROUP = 128
POOL_HIST = 15
CONV_WIDTH = 31
CONV_HIST = 30
D_IN = 2 * D_POOL + 3 * D_CONV
RMS_EPS = 1e-6
LN_EPS = 1e-5

LANES = 128
SUBLANES = 8
BF16_ROWS = 2 * SUBLANES
POOL_PAD = 16
CONV_PAD = 32
TIME_TILE = 512
TILES_PER_STEP = 2
ROW_CHUNK = 64
VMEM_LIMIT_BYTES = 56 * 1024 * 1024

F32 = jnp.float32
BF16 = jnp.bfloat16
U32 = jnp.uint32


def _rms_norm(x, g):
    ms = jnp.mean(x * x, axis=-1, keepdims=True)
    return x * lax.rsqrt(ms + RMS_EPS) * g


def _silu(x):
    return x * jax.nn.sigmoid(x)


def _store_blocked(ext, row0, val):
    for cb in range(val.shape[-1] // LANES):
        ext[cb, pl.ds(row0, val.shape[0]), :] = val[:, cb * LANES:(cb + 1) * LANES]


def _load_blocked(ext, row0, rows):
    return jnp.concatenate([ext[cb, pl.ds(row0, rows), :] for cb in range(ext.shape[0])], axis=-1)


def _pool_rows(uext, row0, rows, cnt=None):
    outs = []
    for gi, w in enumerate(POOL_WINDOWS):
        cur = uext[gi, pl.ds(row0, rows), :]
        span = min(w, SUBLANES)
        acc = cur
        for i in range(1, span):
            acc = acc + uext[gi, pl.ds(row0 - i, rows), :]
        if w > span:
            assert w == 2 * span
            head = uext[gi, pl.ds(row0 - span, span), :]
            for i in range(1, span):
                head = head + uext[gi, pl.ds(row0 - span - i, span), :]
            acc = acc + jnp.concatenate([head, acc[:rows - span]], axis=0)
        if cnt is None:
            outs.append(acc * (1.0 / w) - cur)
        else:
            outs.append(acc / cnt(w) - cur)
    return jnp.concatenate(outs, axis=-1)


def _pack_rows(vext, pk, row0, rows):
    for cb in range(vext.shape[0]):
        for odd in (0, 1):
            x = vext[cb, pl.ds(row0 + odd, rows), :].astype(BF16)
            pk[odd, cb, pl.ds(row0 // 2, rows // 2), :] = pltpu.bitcast(x, U32)


def _conv_rows(pk, row0, rows, dwu_ref, dwb_ref):
    assert CONV_HIST % 2 == 0 and row0 % 2 == 0 and rows % BF16_ROWS == 0
    outs = []
    for cb in range(D_CONV // LANES):
        sl = slice(cb * LANES, (cb + 1) * LANES)
        part = [None] * 4
        for k in range(CONV_WIDTH):
            odd = k % 2
            start = (row0 + k - CONV_HIST - odd) // 2
            x = pltpu.bitcast(pk[odd, cb, pl.ds(start, rows // 2), :], BF16)
            w = pltpu.bitcast(jnp.broadcast_to(dwu_ref[k:k + 1, sl], (rows // 2, LANES)), BF16)
            part[k % 4] = x * w if part[k % 4] is None else part[k % 4] + x * w
        acc = (part[0] + part[1]) + (part[2] + part[3])
        outs.append(acc.astype(F32) + dwb_ref[:, sl])
    return jnp.concatenate(outs, axis=-1)


def _conv_post(c, lng, lnb):
    mu = jnp.mean(c, axis=-1, keepdims=True)
    xc = c - mu
    var = jnp.mean(xc * xc, axis=-1, keepdims=True)
    return _silu(xc * lax.rsqrt(var + LN_EPS) * lng + lnb)


def _dot(a, w):
    return lax.dot_general(a, w, (((1,), (0,)), ((), ())), preferred_element_type=F32)


def _prepare_weights(pmix_ref, dww_ref, bd_ref, dwu_ref):
    half = 2 * POOL_GROUP
    bd_ref[...] = jnp.zeros(bd_ref.shape, bd_ref.dtype)
    for g in range(len(POOL_WINDOWS)):
        lo = (g % 2) * POOL_GROUP
        bd_ref[g // 2, lo:lo + POOL_GROUP, lo:lo + POOL_GROUP] = pmix_ref[g].astype(BF16)
    assert bd_ref.shape == (D_POOL // half, half, half)
    for k in range(CONV_WIDTH):
        tile = jnp.broadcast_to(dww_ref[k:k + 1, :], (BF16_ROWS, D_CONV)).astype(BF16)
        dwu_ref[k:k + 1, :] = pltpu.bitcast(tile, U32)[0:1, :]


def _mix_and_project(x, g_pool, g_conv, pooled_ref, act_ref, cat_ref, bd_ref, ps_ref, pww_ref, pwb_ref,
                     wout_ref, fg_ref):
    half = 2 * POOL_GROUP
    mix = jnp.concatenate([_dot(pooled_ref[:, i * half:(i + 1) * half], bd_ref[i])
                           for i in range(D_POOL // half)], axis=-1)
    cat_ref[:, 0:D_POOL] = (_silu(g_pool) * (mix * ps_ref[...])).astype(BF16)
    cmix = _dot(act_ref[...], pww_ref[...]) + pwb_ref[...]
    cat_ref[:, D_POOL:D_MODEL] = (_silu(g_conv) * cmix).astype(BF16)
    return _rms_norm(x + _dot(cat_ref[...], wout_ref[...]), fg_ref[...])


def _project(x, ng_ref, win_ref):
    h = _rms_norm(x, ng_ref[...]).astype(BF16)

    def proj(lo, hi):
        return _dot(h, win_ref[:, lo:hi])

    u = proj(0, D_POOL)
    g_pool = proj(D_POOL, 2 * D_POOL)
    v = proj(2 * D_POOL, 2 * D_POOL + D_CONV) * jax.nn.sigmoid(proj(2 * D_POOL + D_CONV, 2 * D_POOL + 2 * D_CONV))
    g_conv = proj(2 * D_POOL + 2 * D_CONV, D_IN)
    return u, g_pool, v, g_conv


def _prompt_tile(b, j, last_j, n_streams, x_ref, w, y_ref, sp_ref, sc_ref,
                 uext_ref, vext_ref, pk_ref, pooled_ref, act_ref, cat_ref):
    tt = TIME_TILE

    @pl.when(j == 0)
    def _():
        _store_blocked(uext_ref, 0, jnp.zeros((POOL_PAD, D_POOL), F32))
        _store_blocked(vext_ref, 0, jnp.zeros((CONV_PAD, D_CONV), F32))
        _store_blocked(vext_ref, CONV_PAD + tt, jnp.zeros((SUBLANES, D_CONV), F32))

    x = x_ref[...]
    u, g_pool, v, g_conv = _project(x, w.ng, w.win)
    _store_blocked(uext_ref, POOL_PAD, u)
    _store_blocked(vext_ref, CONV_PAD, v)

    row = lax.broadcasted_iota(jnp.int32, (ROW_CHUNK, LANES), 0)

    def first_cnt(win):
        return jnp.where(j == 0, jnp.minimum(row + 1, win), win).astype(F32)

    for r0 in range(0, tt, ROW_CHUNK):
        pooled = _pool_rows(uext_ref, POOL_PAD + r0, ROW_CHUNK, first_cnt if r0 == 0 else None)
        pooled_ref[r0:r0 + ROW_CHUNK, :] = pooled.astype(BF16)
        lo = 0 if r0 == 0 else CONV_PAD + r0
        _pack_rows(vext_ref, pk_ref, lo, CONV_PAD + r0 + ROW_CHUNK - lo)
        c = _conv_rows(pk_ref, CONV_PAD + r0, ROW_CHUNK, w.dwu, w.dwb)
        act_ref[r0:r0 + ROW_CHUNK, :] = _conv_post(c, w.lng[...], w.lnb[...]).astype(BF16)

    y_ref[...] = _mix_and_project(x, g_pool, g_conv, pooled_ref, act_ref, cat_ref, w.bd, w.ps, w.pww, w.pwb,
                                  w.wout, w.fg)

    for bb in range(n_streams):
        @pl.when((j == last_j) & (b == bb))
        def _(bb=bb):
            sp_ref[:, bb, :] = _load_blocked(uext_ref, POOL_PAD + tt - POOL_HIST, POOL_HIST)
            sc_ref[:, bb, :] = _load_blocked(vext_ref, CONV_PAD + tt - CONV_HIST, CONV_HIST)

    _store_blocked(uext_ref, 0, _load_blocked(uext_ref, tt, POOL_PAD))
    _store_blocked(vext_ref, 0, _load_blocked(vext_ref, tt, CONV_PAD))


def _sample_rows(n_streams, seq, x_ref, cp_ref, cc_ref, w, y_ref, sp_ref, sc_ref,
                 uext_ref, vext_ref, pk_ref, pooled_ref, act_ref, cat_ref):
    x = x_ref[...]
    u, g_pool, v, g_conv = _project(x, w.ng, w.win)

    for s in range(n_streams):
        rows = slice(s * seq, (s + 1) * seq)
        _store_blocked(uext_ref, POOL_PAD - POOL_HIST, cp_ref[:, s, :])
        _store_blocked(uext_ref, POOL_PAD, u[rows, :])
        _store_blocked(vext_ref, CONV_PAD - CONV_HIST, cc_ref[:, s, :])
        _store_blocked(vext_ref, CONV_PAD, v[rows, :])
        pooled_ref[rows, :] = _pool_rows(uext_ref, POOL_PAD, seq).astype(BF16)
        _store_blocked(vext_ref, CONV_PAD + seq, jnp.zeros((SUBLANES, D_CONV), F32))
        _pack_rows(vext_ref, pk_ref, CONV_PAD - CONV_HIST, CONV_HIST + seq)
        c = _conv_rows(pk_ref, CONV_PAD, seq, w.dwu, w.dwb)
        act_ref[rows, :] = _conv_post(c, w.lng[...], w.lnb[...]).astype(BF16)
        sp_ref[:, s, :] = _load_blocked(uext_ref, POOL_PAD + seq - POOL_HIST, POOL_HIST)
        sc_ref[:, s, :] = _load_blocked(vext_ref, CONV_PAD + seq - CONV_HIST, CONV_HIST)

    y_ref[...] = _mix_and_project(x, g_pool, g_conv, pooled_ref.at[0:n_streams * seq], act_ref.at[0:n_streams * seq],
                                  cat_ref.at[0:n_streams * seq], w.bd, w.ps, w.pww, w.pwb, w.wout, w.fg)


class _Weights:
    def __init__(self, ng, win, ps, dwb, lng, lnb, pww, pwb, wout, fg, bd, dwu):
        self.ng, self.win, self.ps, self.dwb, self.lng, self.lnb = ng, win, ps, dwb, lng, lnb
        self.pww, self.pwb, self.wout, self.fg, self.bd, self.dwu = pww, pwb, wout, fg, bd, dwu


def _mixer_kernel(xp_ref, xs_ref, cp_ref, cc_ref, ng_ref, win_ref, pmix_ref, ps_ref, dww_ref, dwb_ref,
                  lng_ref, lnb_ref, pww_ref, pwb_ref, wout_ref, fg_ref,
                  yp_ref, ys_ref, spp_ref, scp_ref, sps_ref, scs_ref,
                  bd_ref, dwu_ref, uext_ref, vext_ref, pk_ref, pooled_ref, act_ref, cat_ref,
                  *, n_steps, tiles_per_stream, n_prompt_streams, n_sample_streams, sample_seq):
    s = pl.program_id(0)
    w = _Weights(ng_ref, win_ref, ps_ref, dwb_ref, lng_ref, lnb_ref, pww_ref, pwb_ref, wout_ref, fg_ref,
                 bd_ref, dwu_ref)

    @pl.when(s == 0)
    def _():
        _prepare_weights(pmix_ref, dww_ref, bd_ref, dwu_ref)

    @pl.when(s < n_steps)
    def _():
        steps_per_stream = tiles_per_stream // TILES_PER_STEP

        def tile(i, carry):
            rows = pl.ds(pl.multiple_of(i * TIME_TILE, TIME_TILE), TIME_TILE)
            _prompt_tile(s // steps_per_stream, lax.rem(s, steps_per_stream) * TILES_PER_STEP + i,
                         tiles_per_stream - 1, n_prompt_streams, xp_ref.at[rows], w, yp_ref.at[rows],
                         spp_ref, scp_ref, uext_ref, vext_ref, pk_ref, pooled_ref, act_ref, cat_ref)
            return carry

        lax.fori_loop(0, TILES_PER_STEP, tile, 0)

    @pl.when(s == n_steps)
    def _():
        _sample_rows(n_sample_streams, sample_seq, xs_ref, cp_ref, cc_ref, w, ys_ref, sps_ref, scs_ref,
                     uext_ref, vext_ref, pk_ref, pooled_ref, act_ref, cat_ref)


def _resident(shape):
    return pl.BlockSpec(shape, lambda s: (0,) * len(shape), pipeline_mode=pl.Buffered(1))


def _whole(shape):
    return pl.BlockSpec(shape, lambda s: (0,) * len(shape))


def kernel(x_prompt, x_sample, cache_pool, cache_conv, norm_g, w_in, pool_mix, pool_scale, dw_w, dw_b,
           ln_g, ln_b, pw_w, pw_b, w_out, final_g):
    batch, seq, d_model = x_prompt.shape
    dec_batch, dec_seq, _ = x_sample.shape
    assert d_model == D_MODEL and w_in.shape == (1, D_MODEL, D_IN)
    step_rows = TILES_PER_STEP * TIME_TILE
    assert seq % step_rows == 0 and dec_seq % BF16_ROWS == 0 and dec_batch * dec_seq <= TIME_TILE
    tiles_per_stream = seq // TIME_TILE
    steps_per_stream = seq // step_rows
    n_steps = batch * steps_per_stream
    rows = dec_batch * dec_seq

    def step_index(s):
        t = jnp.minimum(s, n_steps - 1)
        return (t // steps_per_stream, t % steps_per_stream, 0)

    y_prompt, y_sample, sp_prompt, sc_prompt, sp_sample, sc_sample = pl.pallas_call(
        functools.partial(_mixer_kernel, n_steps=n_steps, tiles_per_stream=tiles_per_stream,
                          n_prompt_streams=batch, n_sample_streams=dec_batch, sample_seq=dec_seq),
        grid=(n_steps + 1,),
        in_specs=[
            pl.BlockSpec((None, step_rows, D_MODEL), step_index),
            _whole((rows, D_MODEL)),
            _whole((POOL_HIST, dec_batch, D_POOL)),
            _whole((CONV_HIST, dec_batch, D_CONV)),
            _resident((1, D_MODEL)),
            _resident((D_MODEL, D_IN)),
            _resident((len(POOL_WINDOWS), POOL_GROUP, POOL_GROUP)),
            _resident((1, D_POOL)),
            _resident((CONV_WIDTH, D_CONV)),
            _resident((1, D_CONV)),
            _resident((1, D_CONV)),
            _resident((1, D_CONV)),
            _resident((D_CONV, D_CONV)),
            _resident((1, D_CONV)),
            _resident((D_MODEL, D_MODEL)),
            _resident((1, D_MODEL)),
        ],
        out_specs=[
            pl.BlockSpec((None, step_rows, D_MODEL), step_index),
            _whole((rows, D_MODEL)),
            _whole((POOL_HIST, batch, D_POOL)),
            _whole((CONV_HIST, batch, D_CONV)),
            _whole((POOL_HIST, dec_batch, D_POOL)),
            _whole((CONV_HIST, dec_batch, D_CONV)),
        ],
        out_shape=[
            jax.ShapeDtypeStruct((batch, seq, D_MODEL), F32),
            jax.ShapeDtypeStruct((rows, D_MODEL), F32),
            jax.ShapeDtypeStruct((POOL_HIST, batch, D_POOL), F32),
            jax.ShapeDtypeStruct((CONV_HIST, batch, D_CONV), F32),
            jax.ShapeDtypeStruct((POOL_HIST, dec_batch, D_POOL), F32),
            jax.ShapeDtypeStruct((CONV_HIST, dec_batch, D_CONV), F32),
        ],
        scratch_shapes=[
            pltpu.VMEM((D_POOL // (2 * POOL_GROUP), 2 * POOL_GROUP, 2 * POOL_GROUP), BF16),
            pltpu.VMEM((CONV_WIDTH, D_CONV), U32),
            pltpu.VMEM((D_POOL // LANES, POOL_PAD + TIME_TILE, LANES), F32),
            pltpu.VMEM((D_CONV // LANES, CONV_PAD + TIME_TILE + SUBLANES, LANES), F32),
            pltpu.VMEM((2, D_CONV // LANES, (CONV_PAD + TIME_TILE) // 2, LANES), U32),
            pltpu.VMEM((TIME_TILE, D_POOL), BF16),
            pltpu.VMEM((TIME_TILE, D_CONV), BF16),
            pltpu.VMEM((TIME_TILE, D_MODEL), BF16),
        ],
        compiler_params=pltpu.CompilerParams(
            dimension_semantics=("arbitrary",), vmem_limit_bytes=VMEM_LIMIT_BYTES),
        name="stream_mixer",
    )(x_prompt, x_sample.reshape(rows, D_MODEL), jnp.swapaxes(cache_pool[0], 0, 1),
      jnp.swapaxes(cache_conv[0], 0, 1), norm_g, w_in[0], pool_mix[0], pool_scale, dw_w[0], dw_b, ln_g, ln_b,
      pw_w[0], pw_b, w_out[0], final_g.reshape(1, D_MODEL))

    def as_state(t):
        return jnp.swapaxes(t, 0, 1)[None]

    return (y_prompt, y_sample.reshape(dec_batch, dec_seq, D_MODEL), as_state(sp_prompt), as_state(sc_prompt),
            as_state(sp_sample), as_state(sc_sample))
```

```python
import functools

import jax
import jax.numpy as jnp
from jax import lax
from jax.experimental import pallas as pl
from jax.experimental.pallas import tpu as pltpu

D_MODEL = 1024
D_POOL = 512
D_CONV = 512
POOL_WINDOWS = (2, 4, 8, 16)
POOL_GROUP = 128
POOL_HIST = 15
CONV_WIDTH = 31
CONV_HIST = 30
D_IN = 2 * D_POOL + 3 * D_CONV
RMS_EPS = 1e-6
LN_EPS = 1e-5

LANES = 128
SUBLANES = 8
BF16_ROWS = 2 * SUBLANES
POOL_PAD = 16
CONV_PAD = 32
TIME_TILE = 512
TILES_PER_STEP = 2
ROW_CHUNK = 64
VMEM_LIMIT_BYTES = 56 * 1024 * 1024

F32 = jnp.float32
BF16 = jnp.bfloat16
U32 = jnp.uint32


def _rms_norm(x, g):
    ms = jnp.mean(x * x, axis=-1, keepdims=True)
    return x * lax.rsqrt(ms + RMS_EPS) * g


def _silu(x):
    return x * jax.nn.sigmoid(x)


def _store_blocked(ext, row0, val):
    for cb in range(val.shape[-1] // LANES):
        ext[cb, pl.ds(row0, val.shape[0]), :] = val[:, cb * LANES:(cb + 1) * LANES]


def _load_blocked(ext, row0, rows):
    return jnp.concatenate([ext[cb, pl.ds(row0, rows), :] for cb in range(ext.shape[0])], axis=-1)


def _pool_rows(uext, row0, rows, cnt=None):
    outs = []
    for gi, w in enumerate(POOL_WINDOWS):
        cur = uext[gi, pl.ds(row0, rows), :]
        span = min(w, SUBLANES)
        acc = cur
        for i in range(1, span):
            acc = acc + uext[gi, pl.ds(row0 - i, rows), :]
        if w > span:
            assert w == 2 * span
            head = uext[gi, pl.ds(row0 - span, span), :]
            for i in range(1, span):
                head = head + uext[gi, pl.ds(row0 - span - i, span), :]
            acc = acc + jnp.concatenate([head, acc[:rows - span]], axis=0)
        if cnt is None:
            outs.append(acc * (1.0 / w) - cur)
        else:
            outs.append(acc / cnt(w) - cur)
    return jnp.concatenate(outs, axis=-1)


def _pack_rows(vext, pk, row0, rows):
    for cb in range(vext.shape[0]):
        for odd in (0, 1):
            x = vext[cb, pl.ds(row0 + odd, rows), :].astype(BF16)
            pk[odd, cb, pl.ds(row0 // 2, rows // 2), :] = pltpu.bitcast(x, U32)


def _conv_rows(pk, row0, rows, dwu_ref, dwb_ref):
    assert CONV_HIST % 2 == 0 and row0 % 2 == 0 and rows % BF16_ROWS == 0
    outs = []
    for cb in range(D_CONV // LANES):
        sl = slice(cb * LANES, (cb + 1) * LANES)
        part = [None] * 4
        for k in range(CONV_WIDTH):
            odd = k % 2
            start = (row0 + k - CONV_HIST - odd) // 2
            x = pltpu.bitcast(pk[odd, cb, pl.ds(start, rows // 2), :], BF16)
            w = pltpu.bitcast(jnp.broadcast_to(dwu_ref[k:k + 1, sl], (rows // 2, LANES)), BF16)
            part[k % 4] = x * w if part[k % 4] is None else part[k % 4] + x * w
        acc = (part[0] + part[1]) + (part[2] + part[3])
        outs.append(acc.astype(F32) + dwb_ref[:, sl])
    return jnp.concatenate(outs, axis=-1)


def _conv_post(c, lng, lnb):
    mu = jnp.mean(c, axis=-1, keepdims=True)
    xc = c - mu
    var = jnp.mean(xc * xc, axis=-1, keepdims=True)
    return _silu(xc * lax.rsqrt(var + LN_EPS) * lng + lnb)


def _dot(a, w):
    return lax.dot_general(a, w, (((1,), (0,)), ((), ())), preferred_element_type=F32)


def _prepare_weights(pmix_ref, dww_ref, bd_ref, dwu_ref):
    half = 2 * POOL_GROUP
    bd_ref[...] = jnp.zeros(bd_ref.shape, bd_ref.dtype)
    for g in range(len(POOL_WINDOWS)):
        lo = (g % 2) * POOL_GROUP
        bd_ref[g // 2, lo:lo + POOL_GROUP, lo:lo + POOL_GROUP] = pmix_ref[g].astype(BF16)
    assert bd_ref.shape == (D_POOL // half, half, half)
    for k in range(CONV_WIDTH):
        tile = jnp.broadcast_to(dww_ref[k:k + 1, :], (BF16_ROWS, D_CONV)).astype(BF16)
        dwu_ref[k:k + 1, :] = pltpu.bitcast(tile, U32)[0:1, :]


def _mix_and_project(x, g_pool, g_conv, pooled_ref, act_ref, cat_ref, bd_ref, ps_ref, pww_ref, pwb_ref,
                     wout_ref, fg_ref):
    half = 2 * POOL_GROUP
    mix = jnp.concatenate([_dot(pooled_ref[:, i * half:(i + 1) * half], bd_ref[i])
                           for i in range(D_POOL // half)], axis=-1)
    cat_ref[:, 0:D_POOL] = (_silu(g_pool) * (mix * ps_ref[...])).astype(BF16)
    cmix = _dot(act_ref[...], pww_ref[...]) + pwb_ref[...]
    cat_ref[:, D_POOL:D_MODEL] = (_silu(g_conv) * cmix).astype(BF16)
    return _rms_norm(x + _dot(cat_ref[...], wout_ref[...]), fg_ref[...])


def _project(x, ng_ref, win_ref):
    h = _rms_norm(x, ng_ref[...]).astype(BF16)

    def proj(lo, hi):
        return _dot(h, win_ref[:, lo:hi])

    u = proj(0, D_POOL)
    g_pool = proj(D_POOL, 2 * D_POOL)
    v = proj(2 * D_POOL, 2 * D_POOL + D_CONV) * jax.nn.sigmoid(proj(2 * D_POOL + D_CONV, 2 * D_POOL + 2 * D_CONV))
    g_conv = proj(2 * D_POOL + 2 * D_CONV, D_IN)
    return u, g_pool, v, g_conv


def _prompt_tile(b, j, last_j, n_streams, may_start, may_end, x_ref, w, y_ref, sp_ref, sc_ref,
                 uext_ref, vext_ref, pk_ref, pooled_ref, act_ref, cat_ref):
    tt = TIME_TILE

    if may_start:
        @pl.when(j == 0)
        def _():
            _store_blocked(uext_ref, 0, jnp.zeros((POOL_PAD, D_POOL), F32))
            _store_blocked(vext_ref, 0, jnp.zeros((CONV_PAD, D_CONV), F32))
            _store_blocked(vext_ref, CONV_PAD + tt, jnp.zeros((SUBLANES, D_CONV), F32))

    x = x_ref[...]
    u, g_pool, v, g_conv = _project(x, w.ng, w.win)
    _store_blocked(uext_ref, POOL_PAD, u)
    _store_blocked(vext_ref, CONV_PAD, v)

    row = lax.broadcasted_iota(jnp.int32, (ROW_CHUNK, LANES), 0)

    def first_cnt(win):
        return jnp.where(j == 0, jnp.minimum(row + 1, win), win).astype(F32)

    for r0 in range(0, tt, ROW_CHUNK):
        pooled = _pool_rows(uext_ref, POOL_PAD + r0, ROW_CHUNK, first_cnt if r0 == 0 and may_start else None)
        pooled_ref[r0:r0 + ROW_CHUNK, :] = pooled.astype(BF16)
        lo = 0 if r0 == 0 else CONV_PAD + r0
        _pack_rows(vext_ref, pk_ref, lo, CONV_PAD + r0 + ROW_CHUNK - lo)
        c = _conv_rows(pk_ref, CONV_PAD + r0, ROW_CHUNK, w.dwu, w.dwb)
        act_ref[r0:r0 + ROW_CHUNK, :] = _conv_post(c, w.lng[...], w.lnb[...]).astype(BF16)

    y_ref[...] = _mix_and_project(x, g_pool, g_conv, pooled_ref, act_ref, cat_ref, w.bd, w.ps, w.pww, w.pwb,
                                  w.wout, w.fg)

    for bb in range(n_streams if may_end else 0):
        @pl.when((j == last_j) & (b == bb))
        def _(bb=bb):
            sp_ref[:, bb, :] = _load_blocked(uext_ref, POOL_PAD + tt - POOL_HIST, POOL_HIST)
            sc_ref[:, bb, :] = _load_blocked(vext_ref, CONV_PAD + tt - CONV_HIST, CONV_HIST)

    _store_blocked(uext_ref, 0, _load_blocked(uext_ref, tt, POOL_PAD))
    _store_blocked(vext_ref, 0, _load_blocked(vext_ref, tt, CONV_PAD))


def _sample_rows(n_streams, seq, x_ref, cp_ref, cc_ref, w, y_ref, sp_ref, sc_ref,
                 uext_ref, vext_ref, pk_ref, pooled_ref, act_ref, cat_ref):
    x = x_ref[...]
    u, g_pool, v, g_conv = _project(x, w.ng, w.win)

    for s in range(n_streams):
        rows = slice(s * seq, (s + 1) * seq)
        _store_blocked(uext_ref, POOL_PAD - POOL_HIST, cp_ref[:, s, :])
        _store_blocked(uext_ref, POOL_PAD, u[rows, :])
        _store_blocked(vext_ref, CONV_PAD - CONV_HIST, cc_ref[:, s, :])
        _store_blocked(vext_ref, CONV_PAD, v[rows, :])
        pooled_ref[rows, :] = _pool_rows(uext_ref, POOL_PAD, seq).astype(BF16)
        _store_blocked(vext_ref, CONV_PAD + seq, jnp.zeros((SUBLANES, D_CONV), F32))
        _pack_rows(vext_ref, pk_ref, CONV_PAD - CONV_HIST, CONV_HIST + seq)
        c = _conv_rows(pk_ref, CONV_PAD, seq, w.dwu, w.dwb)
        act_ref[rows, :] = _conv_post(c, w.lng[...], w.lnb[...]).astype(BF16)
        sp_ref[:, s, :] = _load_blocked(uext_ref, POOL_PAD + seq - POOL_HIST, POOL_HIST)
        sc_ref[:, s, :] = _load_blocked(vext_ref, CONV_PAD + seq - CONV_HIST, CONV_HIST)

    y_ref[...] = _mix_and_project(x, g_pool, g_conv, pooled_ref.at[0:n_streams * seq], act_ref.at[0:n_streams * seq],
                                  cat_ref.at[0:n_streams * seq], w.bd, w.ps, w.pww, w.pwb, w.wout, w.fg)


class _Weights:
    def __init__(self, ng, win, ps, dwb, lng, lnb, pww, pwb, wout, fg, bd, dwu):
        self.ng, self.win, self.ps, self.dwb, self.lng, self.lnb = ng, win, ps, dwb, lng, lnb
        self.pww, self.pwb, self.wout, self.fg, self.bd, self.dwu = pww, pwb, wout, fg, bd, dwu


def _mixer_kernel(xp_ref, xs_ref, cp_ref, cc_ref, ng_ref, win_ref, pmix_ref, ps_ref, dww_ref, dwb_ref,
                  lng_ref, lnb_ref, pww_ref, pwb_ref, wout_ref, fg_ref,
                  yp_ref, ys_ref, spp_ref, scp_ref, sps_ref, scs_ref,
                  bd_ref, dwu_ref, uext_ref, vext_ref, pk_ref, pooled_ref, act_ref, cat_ref,
                  *, n_steps, tiles_per_stream, n_prompt_streams, n_sample_streams, sample_seq):
    s = pl.program_id(0)
    w = _Weights(ng_ref, win_ref, ps_ref, dwb_ref, lng_ref, lnb_ref, pww_ref, pwb_ref, wout_ref, fg_ref,
                 bd_ref, dwu_ref)

    @pl.when(s == 0)
    def _():
        _prepare_weights(pmix_ref, dww_ref, bd_ref, dwu_ref)

    @pl.when(s < n_steps)
    def _():
        steps_per_stream = tiles_per_stream // TILES_PER_STEP

        for i in range(TILES_PER_STEP):
            rows = pl.ds(i * TIME_TILE, TIME_TILE)
            _prompt_tile(s // steps_per_stream, lax.rem(s, steps_per_stream) * TILES_PER_STEP + i,
                         tiles_per_stream - 1, n_prompt_streams, i == 0, i == TILES_PER_STEP - 1,
                         xp_ref.at[rows], w, yp_ref.at[rows],
                         spp_ref, scp_ref, uext_ref, vext_ref, pk_ref, pooled_ref, act_ref, cat_ref)

    @pl.when(s == n_steps)
    def _():
        _sample_rows(n_sample_streams, sample_seq, xs_ref, cp_ref, cc_ref, w, ys_ref, sps_ref, scs_ref,
                     uext_ref, vext_ref, pk_ref, pooled_ref, act_ref, cat_ref)


def _resident(shape):
    return pl.BlockSpec(shape, lambda s: (0,) * len(shape), pipeline_mode=pl.Buffered(1))


def _whole(shape):
    return pl.BlockSpec(shape, lambda s: (0,) * len(shape))


def kernel(x_prompt, x_sample, cache_pool, cache_conv, norm_g, w_in, pool_mix, pool_scale, dw_w, dw_b,
           ln_g, ln_b, pw_w, pw_b, w_out, final_g):
    batch, seq, d_model = x_prompt.shape
    dec_batch, dec_seq, _ = x_sample.shape
    assert d_model == D_MODEL and w_in.shape == (1, D_MODEL, D_IN)
    step_rows = TILES_PER_STEP * TIME_TILE
    assert seq % step_rows == 0 and dec_seq % BF16_ROWS == 0 and dec_batch * dec_seq <= TIME_TILE
    tiles_per_stream = seq // TIME_TILE
    steps_per_stream = seq // step_rows
    n_steps = batch * steps_per_stream
    rows = dec_batch * dec_seq

    def step_index(s):
        t = jnp.minimum(s, n_steps - 1)
        return (t // steps_per_stream, t % steps_per_stream, 0)

    y_prompt, y_sample, sp_prompt, sc_prompt, sp_sample, sc_sample = pl.pallas_call(
        functools.partial(_mixer_kernel, n_steps=n_steps, tiles_per_stream=tiles_per_stream,
                          n_prompt_streams=batch, n_sample_streams=dec_batch, sample_seq=dec_seq),
        grid=(n_steps + 1,),
        in_specs=[
            pl.BlockSpec((None, step_rows, D_MODEL), step_index),
            _whole((rows, D_MODEL)),
            _whole((POOL_HIST, dec_batch, D_POOL)),
            _whole((CONV_HIST, dec_batch, D_CONV)),
            _resident((1, D_MODEL)),
            _resident((D_MODEL, D_IN)),
            _resident((len(POOL_WINDOWS), POOL_GROUP, POOL_GROUP)),
            _resident((1, D_POOL)),
            _resident((CONV_WIDTH, D_CONV)),
            _resident((1, D_CONV)),
            _resident((1, D_CONV)),
            _resident((1, D_CONV)),
            _resident((D_CONV, D_CONV)),
            _resident((1, D_CONV)),
            _resident((D_MODEL, D_MODEL)),
            _resident((1, D_MODEL)),
        ],
        out_specs=[
            pl.BlockSpec((None, step_rows, D_MODEL), step_index),
            _whole((rows, D_MODEL)),
            _whole((POOL_HIST, batch, D_POOL)),
            _whole((CONV_HIST, batch, D_CONV)),
            _whole((POOL_HIST, dec_batch, D_POOL)),
            _whole((CONV_HIST, dec_batch, D_CONV)),
        ],
        out_shape=[
            jax.ShapeDtypeStruct((batch, seq, D_MODEL), F32),
            jax.ShapeDtypeStruct((rows, D_MODEL), F32),
            jax.ShapeDtypeStruct((POOL_HIST, batch, D_POOL), F32),
            jax.ShapeDtypeStruct((CONV_HIST, batch, D_CONV), F32),
            jax.ShapeDtypeStruct((POOL_HIST, dec_batch, D_POOL), F32),
            jax.ShapeDtypeStruct((CONV_HIST, dec_batch, D_CONV), F32),
        ],
        scratch_shapes=[
            pltpu.VMEM((D_POOL // (2 * POOL_GROUP), 2 * POOL_GROUP, 2 * POOL_GROUP), BF16),
            pltpu.VMEM((CONV_WIDTH, D_CONV), U32),
            pltpu.VMEM((D_POOL // LANES, POOL_PAD + TIME_TILE, LANES), F32),
            pltpu.VMEM((D_CONV // LANES, CONV_PAD + TIME_TILE + SUBLANES, LANES), F32),
            pltpu.VMEM((2, D_CONV // LANES, (CONV_PAD + TIME_TILE) // 2, LANES), U32),
            pltpu.VMEM((TIME_TILE, D_POOL), BF16),
            pltpu.VMEM((TIME_TILE, D_CONV), BF16),
            pltpu.VMEM((TIME_TILE, D_MODEL), BF16),
        ],
        compiler_params=pltpu.CompilerParams(
            dimension_semantics=("arbitrary",), vmem_limit_bytes=VMEM_LIMIT_BYTES),
        name="stream_mixer",
    )(x_prompt, x_sample.reshape(rows, D_MODEL), jnp.swapaxes(cache_pool[0], 0, 1),
      jnp.swapaxes(cache_conv[0], 0, 1), norm_g, w_in[0], pool_mix[0], pool_scale, dw_w[0], dw_b, ln_g, ln_b,
      pw_w[0], pw_b, w_out[0], final_g.reshape(1, D_MODEL))

    def as_state(t):
        return jnp.swapaxes(t, 0, 1)[None]

    return (y_prompt, y_sample.reshape(dec_batch, dec_seq, D_MODEL), as_state(sp_prompt), as_state(sc_prompt),
            as_state(sp_sample), as_state(sc_sample))
```

```python
import functools

import jax
import jax.numpy as jnp
from jax import lax
from jax.experimental import pallas as pl
from jax.experimental.pallas import tpu as pltpu

D_MODEL = 1024
D_POOL = 512
D_CONV = 512
POOL_WINDOWS = (2, 4, 8, 16)
POOL_GROUP = 128
POOL_HIST = 15
CONV_WIDTH = 31
CONV_HIST = 30
D_IN = 2 * D_POOL + 3 * D_CONV
RMS_EPS = 1e-6
LN_EPS = 1e-5

LANES = 128
SUBLANES = 8
BF16_ROWS = 2 * SUBLANES
POOL_PAD = 16
CONV_PAD = 32
TIME_TILE = 1024
TILES_PER_STEP = 1
ROW_CHUNK = 64
VMEM_LIMIT_BYTES = 56 * 1024 * 1024

F32 = jnp.float32
BF16 = jnp.bfloat16
U32 = jnp.uint32


def _rms_norm(x, g):
    ms = jnp.mean(x * x, axis=-1, keepdims=True)
    return x * lax.rsqrt(ms + RMS_EPS) * g


def _silu(x):
    return x * jax.nn.sigmoid(x)


def _store_blocked(ext, row0, val):
    for cb in range(val.shape[-1] // LANES):
        ext[cb, pl.ds(row0, val.shape[0]), :] = val[:, cb * LANES:(cb + 1) * LANES]


def _load_blocked(ext, row0, rows):
    return jnp.concatenate([ext[cb, pl.ds(row0, rows), :] for cb in range(ext.shape[0])], axis=-1)


def _pool_rows(uext, row0, rows, cnt=None):
    outs = []
    for gi, w in enumerate(POOL_WINDOWS):
        cur = uext[gi, pl.ds(row0, rows), :]
        span = min(w, SUBLANES)
        acc = cur
        for i in range(1, span):
            acc = acc + uext[gi, pl.ds(row0 - i, rows), :]
        if w > span:
            assert w == 2 * span
            head = uext[gi, pl.ds(row0 - span, span), :]
            for i in range(1, span):
                head = head + uext[gi, pl.ds(row0 - span - i, span), :]
            acc = acc + jnp.concatenate([head, acc[:rows - span]], axis=0)
        if cnt is None:
            outs.append(acc * (1.0 / w) - cur)
        else:
            outs.append(acc / cnt(w) - cur)
    return jnp.concatenate(outs, axis=-1)


def _pack_rows(vext, pk, row0, rows):
    for cb in range(vext.shape[0]):
        for odd in (0, 1):
            x = vext[cb, pl.ds(row0 + odd, rows), :].astype(BF16)
            pk[odd, cb, pl.ds(row0 // 2, rows // 2), :] = pltpu.bitcast(x, U32)


def _conv_rows(pk, row0, rows, dwu_ref, dwb_ref):
    assert CONV_HIST % 2 == 0 and row0 % 2 == 0 and rows % BF16_ROWS == 0
    outs = []
    for cb in range(D_CONV // LANES):
        sl = slice(cb * LANES, (cb + 1) * LANES)
        part = [None] * 4
        for k in range(CONV_WIDTH):
            odd = k % 2
            start = (row0 + k - CONV_HIST - odd) // 2
            x = pltpu.bitcast(pk[odd, cb, pl.ds(start, rows // 2), :], BF16)
            w = pltpu.bitcast(jnp.broadcast_to(dwu_ref[k:k + 1, sl], (rows // 2, LANES)), BF16)
            part[k % 4] = x * w if part[k % 4] is None else part[k % 4] + x * w
        acc = (part[0] + part[1]) + (part[2] + part[3])
        outs.append(acc.astype(F32) + dwb_ref[:, sl])
    return jnp.concatenate(outs, axis=-1)


def _conv_post(c, lng, lnb):
    mu = jnp.mean(c, axis=-1, keepdims=True)
    xc = c - mu
    var = jnp.mean(xc * xc, axis=-1, keepdims=True)
    return _silu(xc * lax.rsqrt(var + LN_EPS) * lng + lnb)


def _dot(a, w):
    return lax.dot_general(a, w, (((1,), (0,)), ((), ())), preferred_element_type=F32)


def _prepare_weights(pmix_ref, dww_ref, bd_ref, dwu_ref):
    half = 2 * POOL_GROUP
    bd_ref[...] = jnp.zeros(bd_ref.shape, bd_ref.dtype)
    for g in range(len(POOL_WINDOWS)):
        lo = (g % 2) * POOL_GROUP
        bd_ref[g // 2, lo:lo + POOL_GROUP, lo:lo + POOL_GROUP] = pmix_ref[g].astype(BF16)
    assert bd_ref.shape == (D_POOL // half, half, half)
    for k in range(CONV_WIDTH):
        tile = jnp.broadcast_to(dww_ref[k:k + 1, :], (BF16_ROWS, D_CONV)).astype(BF16)
        dwu_ref[k:k + 1, :] = pltpu.bitcast(tile, U32)[0:1, :]


def _mix_and_project(x, g_pool, g_conv, pooled_ref, act_ref, cat_ref, bd_ref, ps_ref, pww_ref, pwb_ref,
                     wout_ref, fg_ref):
    half = 2 * POOL_GROUP
    mix = jnp.concatenate([_dot(pooled_ref[:, i * half:(i + 1) * half], bd_ref[i])
                           for i in range(D_POOL // half)], axis=-1)
    cat_ref[:, 0:D_POOL] = (_silu(g_pool) * (mix * ps_ref[...])).astype(BF16)
    cmix = _dot(act_ref[...], pww_ref[...]) + pwb_ref[...]
    cat_ref[:, D_POOL:D_MODEL] = (_silu(g_conv) * cmix).astype(BF16)
    return _rms_norm(x + _dot(cat_ref[...], wout_ref[...]), fg_ref[...])


def _project(x, ng_ref, win_ref):
    h = _rms_norm(x, ng_ref[...]).astype(BF16)

    def proj(lo, hi):
        return _dot(h, win_ref[:, lo:hi])

    u = proj(0, D_POOL)
    g_pool = proj(D_POOL, 2 * D_POOL)
    v = proj(2 * D_POOL, 2 * D_POOL + D_CONV) * jax.nn.sigmoid(proj(2 * D_POOL + D_CONV, 2 * D_POOL + 2 * D_CONV))
    g_conv = proj(2 * D_POOL + 2 * D_CONV, D_IN)
    return u, g_pool, v, g_conv


def _prompt_tile(b, j, last_j, n_streams, may_start, may_end, x_ref, w, y_ref, sp_ref, sc_ref,
                 uext_ref, vext_ref, pk_ref, pooled_ref, act_ref, cat_ref):
    tt = TIME_TILE

    if may_start:
        @pl.when(j == 0)
        def _():
            _store_blocked(uext_ref, 0, jnp.zeros((POOL_PAD, D_POOL), F32))
            _store_blocked(vext_ref, 0, jnp.zeros((CONV_PAD, D_CONV), F32))
            _store_blocked(vext_ref, CONV_PAD + tt, jnp.zeros((SUBLANES, D_CONV), F32))

    x = x_ref[...]
    u, g_pool, v, g_conv = _project(x, w.ng, w.win)
    _store_blocked(uext_ref, POOL_PAD, u)
    _store_blocked(vext_ref, CONV_PAD, v)

    row = lax.broadcasted_iota(jnp.int32, (ROW_CHUNK, LANES), 0)

    def first_cnt(win):
        return jnp.where(j == 0, jnp.minimum(row + 1, win), win).astype(F32)

    for r0 in range(0, tt, ROW_CHUNK):
        pooled = _pool_rows(uext_ref, POOL_PAD + r0, ROW_CHUNK, first_cnt if r0 == 0 and may_start else None)
        pooled_ref[r0:r0 + ROW_CHUNK, :] = pooled.astype(BF16)
        lo = 0 if r0 == 0 else CONV_PAD + r0
        _pack_rows(vext_ref, pk_ref, lo, CONV_PAD + r0 + ROW_CHUNK - lo)
        c = _conv_rows(pk_ref, CONV_PAD + r0, ROW_CHUNK, w.dwu, w.dwb)
        act_ref[r0:r0 + ROW_CHUNK, :] = _conv_post(c, w.lng[...], w.lnb[...]).astype(BF16)

    y_ref[...] = _mix_and_project(x, g_pool, g_conv, pooled_ref, act_ref, cat_ref, w.bd, w.ps, w.pww, w.pwb,
                                  w.wout, w.fg)

    for bb in range(n_streams if may_end else 0):
        @pl.when((j == last_j) & (b == bb))
        def _(bb=bb):
            sp_ref[:, bb, :] = _load_blocked(uext_ref, POOL_PAD + tt - POOL_HIST, POOL_HIST)
            sc_ref[:, bb, :] = _load_blocked(vext_ref, CONV_PAD + tt - CONV_HIST, CONV_HIST)

    _store_blocked(uext_ref, 0, _load_blocked(uext_ref, tt, POOL_PAD))
    _store_blocked(vext_ref, 0, _load_blocked(vext_ref, tt, CONV_PAD))


def _sample_rows(n_streams, seq, x_ref, cp_ref, cc_ref, w, y_ref, sp_ref, sc_ref,
                 uext_ref, vext_ref, pk_ref, pooled_ref, act_ref, cat_ref):
    x = x_ref[...]
    u, g_pool, v, g_conv = _project(x, w.ng, w.win)

    for s in range(n_streams):
        rows = slice(s * seq, (s + 1) * seq)
        _store_blocked(uext_ref, POOL_PAD - POOL_HIST, cp_ref[:, s, :])
        _store_blocked(uext_ref, POOL_PAD, u[rows, :])
        _store_blocked(vext_ref, CONV_PAD - CONV_HIST, cc_ref[:, s, :])
        _store_blocked(vext_ref, CONV_PAD, v[rows, :])
        pooled_ref[rows, :] = _pool_rows(uext_ref, POOL_PAD, seq).astype(BF16)
        _store_blocked(vext_ref, CONV_PAD + seq, jnp.zeros((SUBLANES, D_CONV), F32))
        _pack_rows(vext_ref, pk_ref, CONV_PAD - CONV_HIST, CONV_HIST + seq)
        c = _conv_rows(pk_ref, CONV_PAD, seq, w.dwu, w.dwb)
        act_ref[rows, :] = _conv_post(c, w.lng[...], w.lnb[...]).astype(BF16)
        sp_ref[:, s, :] = _load_blocked(uext_ref, POOL_PAD + seq - POOL_HIST, POOL_HIST)
        sc_ref[:, s, :] = _load_blocked(vext_ref, CONV_PAD + seq - CONV_HIST, CONV_HIST)

    y_ref[...] = _mix_and_project(x, g_pool, g_conv, pooled_ref.at[0:n_streams * seq], act_ref.at[0:n_streams * seq],
                                  cat_ref.at[0:n_streams * seq], w.bd, w.ps, w.pww, w.pwb, w.wout, w.fg)


class _Weights:
    def __init__(self, ng, win, ps, dwb, lng, lnb, pww, pwb, wout, fg, bd, dwu):
        self.ng, self.win, self.ps, self.dwb, self.lng, self.lnb = ng, win, ps, dwb, lng, lnb
        self.pww, self.pwb, self.wout, self.fg, self.bd, self.dwu = pww, pwb, wout, fg, bd, dwu


def _mixer_kernel(xp_ref, xs_ref, cp_ref, cc_ref, ng_ref, win_ref, pmix_ref, ps_ref, dww_ref, dwb_ref,
                  lng_ref, lnb_ref, pww_ref, pwb_ref, wout_ref, fg_ref,
                  yp_ref, ys_ref, spp_ref, scp_ref, sps_ref, scs_ref,
                  bd_ref, dwu_ref, uext_ref, vext_ref, pk_ref, pooled_ref, act_ref, cat_ref,
                  *, n_steps, tiles_per_stream, n_prompt_streams, n_sample_streams, sample_seq):
    s = pl.program_id(0)
    w = _Weights(ng_ref, win_ref, ps_ref, dwb_ref, lng_ref, lnb_ref, pww_ref, pwb_ref, wout_ref, fg_ref,
                 bd_ref, dwu_ref)

    @pl.when(s == 0)
    def _():
        _prepare_weights(pmix_ref, dww_ref, bd_ref, dwu_ref)

    @pl.when(s < n_steps)
    def _():
        steps_per_stream = tiles_per_stream // TILES_PER_STEP

        for i in range(TILES_PER_STEP):
            rows = pl.ds(i * TIME_TILE, TIME_TILE)
            _prompt_tile(s // steps_per_stream, lax.rem(s, steps_per_stream) * TILES_PER_STEP + i,
                         tiles_per_stream - 1, n_prompt_streams, i == 0, i == TILES_PER_STEP - 1,
                         xp_ref.at[rows], w, yp_ref.at[rows],
                         spp_ref, scp_ref, uext_ref, vext_ref, pk_ref, pooled_ref, act_ref, cat_ref)

    @pl.when(s == n_steps)
    def _():
        _sample_rows(n_sample_streams, sample_seq, xs_ref, cp_ref, cc_ref, w, ys_ref, sps_ref, scs_ref,
                     uext_ref, vext_ref, pk_ref, pooled_ref, act_ref, cat_ref)


def _resident(shape):
    return pl.BlockSpec(shape, lambda s: (0,) * len(shape), pipeline_mode=pl.Buffered(1))


def _whole(shape):
    return pl.BlockSpec(shape, lambda s: (0,) * len(shape))


def kernel(x_prompt, x_sample, cache_pool, cache_conv, norm_g, w_in, pool_mix, pool_scale, dw_w, dw_b,
           ln_g, ln_b, pw_w, pw_b, w_out, final_g):
    batch, seq, d_model = x_prompt.shape
    dec_batch, dec_seq, _ = x_sample.shape
    assert d_model == D_MODEL and w_in.shape == (1, D_MODEL, D_IN)
    step_rows = TILES_PER_STEP * TIME_TILE
    assert seq % step_rows == 0 and dec_seq % BF16_ROWS == 0 and dec_batch * dec_seq <= TIME_TILE
    tiles_per_stream = seq // TIME_TILE
    steps_per_stream = seq // step_rows
    n_steps = batch * steps_per_stream
    rows = dec_batch * dec_seq

    def step_index(s):
        t = jnp.minimum(s, n_steps - 1)
        return (t // steps_per_stream, t % steps_per_stream, 0)

    y_prompt, y_sample, sp_prompt, sc_prompt, sp_sample, sc_sample = pl.pallas_call(
        functools.partial(_mixer_kernel, n_steps=n_steps, tiles_per_stream=tiles_per_stream,
                          n_prompt_streams=batch, n_sample_streams=dec_batch, sample_seq=dec_seq),
        grid=(n_steps + 1,),
        in_specs=[
            pl.BlockSpec((None, step_rows, D_MODEL), step_index),
            _whole((rows, D_MODEL)),
            _whole((POOL_HIST, dec_batch, D_POOL)),
            _whole((CONV_HIST, dec_batch, D_CONV)),
            _resident((1, D_MODEL)),
            _resident((D_MODEL, D_IN)),
            _resident((len(POOL_WINDOWS), POOL_GROUP, POOL_GROUP)),
            _resident((1, D_POOL)),
            _resident((CONV_WIDTH, D_CONV)),
            _resident((1, D_CONV)),
            _resident((1, D_CONV)),
            _resident((1, D_CONV)),
            _resident((D_CONV, D_CONV)),
            _resident((1, D_CONV)),
            _resident((D_MODEL, D_MODEL)),
            _resident((1, D_MODEL)),
        ],
        out_specs=[
            pl.BlockSpec((None, step_rows, D_MODEL), step_index),
            _whole((rows, D_MODEL)),
            _whole((POOL_HIST, batch, D_POOL)),
            _whole((CONV_HIST, batch, D_CONV)),
            _whole((POOL_HIST, dec_batch, D_POOL)),
            _whole((CONV_HIST, dec_batch, D_CONV)),
        ],
        out_shape=[
            jax.ShapeDtypeStruct((batch, seq, D_MODEL), F32),
            jax.ShapeDtypeStruct((rows, D_MODEL), F32),
            jax.ShapeDtypeStruct((POOL_HIST, batch, D_POOL), F32),
            jax.ShapeDtypeStruct((CONV_HIST, batch, D_CONV), F32),
            jax.ShapeDtypeStruct((POOL_HIST, dec_batch, D_POOL), F32),
            jax.ShapeDtypeStruct((CONV_HIST, dec_batch, D_CONV), F32),
        ],
        scratch_shapes=[
            pltpu.VMEM((D_POOL // (2 * POOL_GROUP), 2 * POOL_GROUP, 2 * POOL_GROUP), BF16),
            pltpu.VMEM((CONV_WIDTH, D_CONV), U32),
            pltpu.VMEM((D_POOL // LANES, POOL_PAD + TIME_TILE, LANES), F32),
            pltpu.VMEM((D_CONV // LANES, CONV_PAD + TIME_TILE + SUBLANES, LANES), F32),
            pltpu.VMEM((2, D_CONV // LANES, (CONV_PAD + TIME_TILE) // 2, LANES), U32),
            pltpu.VMEM((TIME_TILE, D_POOL), BF16),
            pltpu.VMEM((TIME_TILE, D_CONV), BF16),
            pltpu.VMEM((TIME_TILE, D_MODEL), BF16),
        ],
        compiler_params=pltpu.CompilerParams(
            dimension_semantics=("arbitrary",), vmem_limit_bytes=VMEM_LIMIT_BYTES),
        name="stream_mixer",
    )(x_prompt, x_sample.reshape(rows, D_MODEL), jnp.swapaxes(cache_pool[0], 0, 1),
      jnp.swapaxes(cache_conv[0], 0, 1), norm_g, w_in[0], pool_mix[0], pool_scale, dw_w[0], dw_b, ln_g, ln_b,
      pw_w[0], pw_b, w_out[0], final_g.reshape(1, D_MODEL))

    def as_state(t):
        return jnp.swapaxes(t, 0, 1)[None]

    return (y_prompt, y_sample.reshape(dec_batch, dec_seq, D_MODEL), as_state(sp_prompt), as_state(sc_prompt),
            as_state(sp_sample), as_state(sc_sample))
```

```python
import functools

import jax
import jax.numpy as jnp
from jax import lax
from jax.experimental import pallas as pl
from jax.experimental.pallas import tpu as pltpu

D_MODEL = 1024
D_POOL = 512
D_CONV = 512
POOL_WINDOWS = (2, 4, 8, 16)
POOL_GROUP = 128
POOL_HIST = 15
CONV_WIDTH = 31
CONV_HIST = 30
D_IN = 2 * D_POOL + 3 * D_CONV
RMS_EPS = 1e-6
LN_EPS = 1e-5

LANES = 128
SUBLANES = 8
BF16_ROWS = 2 * SUBLANES
POOL_PAD = 16
CONV_PAD = 32
TIME_TILE = 1024
TILES_PER_STEP = 1
ROW_CHUNK = 64
WIN_PIECE_ROWS = 256
N_WIN_PIECES = D_MODEL // WIN_PIECE_ROWS
N_WEIGHT_COPIES = N_WIN_PIECES + 2
VMEM_LIMIT_BYTES = 58 * 1024 * 1024

F32 = jnp.float32
BF16 = jnp.bfloat16
U32 = jnp.uint32


def _rms_norm(x, g):
    ms = jnp.mean(x * x, axis=-1, keepdims=True)
    return x * lax.rsqrt(ms + RMS_EPS) * g


def _silu(x):
    return x * jax.nn.sigmoid(x)


def _store_blocked(ext, row0, val):
    for cb in range(val.shape[-1] // LANES):
        ext[cb, pl.ds(row0, val.shape[0]), :] = val[:, cb * LANES:(cb + 1) * LANES]


def _load_blocked(ext, row0, rows):
    return jnp.concatenate([ext[cb, pl.ds(row0, rows), :] for cb in range(ext.shape[0])], axis=-1)


def _pool_rows(uext, row0, rows, cnt=None):
    outs = []
    for gi, w in enumerate(POOL_WINDOWS):
        cur = uext[gi, pl.ds(row0, rows), :]
        span = min(w, SUBLANES)
        acc = cur
        for i in range(1, span):
            acc = acc + uext[gi, pl.ds(row0 - i, rows), :]
        if w > span:
            assert w == 2 * span
            head = uext[gi, pl.ds(row0 - span, span), :]
            for i in range(1, span):
                head = head + uext[gi, pl.ds(row0 - span - i, span), :]
            acc = acc + jnp.concatenate([head, acc[:rows - span]], axis=0)
        if cnt is None:
            outs.append(acc * (1.0 / w) - cur)
        else:
            outs.append(acc / cnt(w) - cur)
    return jnp.concatenate(outs, axis=-1)


def _pack_rows(vext, pk, row0, rows):
    for cb in range(vext.shape[0]):
        for odd in (0, 1):
            x = vext[cb, pl.ds(row0 + odd, rows), :].astype(BF16)
            pk[odd, cb, pl.ds(row0 // 2, rows // 2), :] = pltpu.bitcast(x, U32)


def _conv_rows(pk, row0, rows, dwu_ref, dwb_ref):
    assert CONV_HIST % 2 == 0 and row0 % 2 == 0 and rows % BF16_ROWS == 0
    outs = []
    for cb in range(D_CONV // LANES):
        sl = slice(cb * LANES, (cb + 1) * LANES)
        part = [None] * 4
        for k in range(CONV_WIDTH):
            odd = k % 2
            start = (row0 + k - CONV_HIST - odd) // 2
            x = pltpu.bitcast(pk[odd, cb, pl.ds(start, rows // 2), :], BF16)
            w = pltpu.bitcast(jnp.broadcast_to(dwu_ref[k:k + 1, sl], (rows // 2, LANES)), BF16)
            part[k % 4] = x * w if part[k % 4] is None else part[k % 4] + x * w
        acc = (part[0] + part[1]) + (part[2] + part[3])
        outs.append(acc.astype(F32) + dwb_ref[:, sl])
    return jnp.concatenate(outs, axis=-1)


def _conv_post(c, lng, lnb):
    mu = jnp.mean(c, axis=-1, keepdims=True)
    xc = c - mu
    var = jnp.mean(xc * xc, axis=-1, keepdims=True)
    return _silu(xc * lax.rsqrt(var + LN_EPS) * lng + lnb)


def _dot(a, w):
    return lax.dot_general(a, w, (((1,), (0,)), ((), ())), preferred_element_type=F32)


def _prepare_weights(pmix_ref, dww_ref, bd_ref, dwu_ref):
    half = 2 * POOL_GROUP
    bd_ref[...] = jnp.zeros(bd_ref.shape, bd_ref.dtype)
    for g in range(len(POOL_WINDOWS)):
        lo = (g % 2) * POOL_GROUP
        bd_ref[g // 2, lo:lo + POOL_GROUP, lo:lo + POOL_GROUP] = pmix_ref[g].astype(BF16)
    assert bd_ref.shape == (D_POOL // half, half, half)
    for k in range(CONV_WIDTH):
        tile = jnp.broadcast_to(dww_ref[k:k + 1, :], (BF16_ROWS, D_CONV)).astype(BF16)
        dwu_ref[k:k + 1, :] = pltpu.bitcast(tile, U32)[0:1, :]


def _no_wait(copy_index):
    del copy_index


def _mix_and_project(x, g_pool, g_conv, pooled_ref, act_ref, cat_ref, bd_ref, ps_ref, pww_ref, pwb_ref,
                     wout_ref, fg_ref, await_weight=_no_wait):
    half = 2 * POOL_GROUP
    mix = jnp.concatenate([_dot(pooled_ref[:, i * half:(i + 1) * half], bd_ref[i])
                           for i in range(D_POOL // half)], axis=-1)
    cat_ref[:, 0:D_POOL] = (_silu(g_pool) * (mix * ps_ref[...])).astype(BF16)
    await_weight(N_WIN_PIECES)
    cmix = _dot(act_ref[...], pww_ref[...]) + pwb_ref[...]
    cat_ref[:, D_POOL:D_MODEL] = (_silu(g_conv) * cmix).astype(BF16)
    await_weight(N_WIN_PIECES + 1)
    return _rms_norm(x + _dot(cat_ref[...], wout_ref[...]), fg_ref[...])


def _project(x, ng_ref, win_ref):
    h = _rms_norm(x, ng_ref[...]).astype(BF16)

    def proj(lo, hi):
        return _dot(h, win_ref[:, lo:hi])

    u = proj(0, D_POOL)
    g_pool = proj(D_POOL, 2 * D_POOL)
    v = proj(2 * D_POOL, 2 * D_POOL + D_CONV) * jax.nn.sigmoid(proj(2 * D_POOL + D_CONV, 2 * D_POOL + 2 * D_CONV))
    g_conv = proj(2 * D_POOL + 2 * D_CONV, D_IN)
    return u, g_pool, v, g_conv


def _project_by_rows(x, ng_ref, win_ref, await_weight):
    h = _rms_norm(x, ng_ref[...]).astype(BF16)
    p = None
    for piece in range(N_WIN_PIECES):
        await_weight(piece)
        rows = slice(piece * WIN_PIECE_ROWS, (piece + 1) * WIN_PIECE_ROWS)
        part = _dot(h[:, rows], win_ref[rows, :])
        p = part if p is None else p + part
    u = p[:, 0:D_POOL]
    g_pool = p[:, D_POOL:2 * D_POOL]
    v = p[:, 2 * D_POOL:2 * D_POOL + D_CONV] * jax.nn.sigmoid(p[:, 2 * D_POOL + D_CONV:2 * D_POOL + 2 * D_CONV])
    g_conv = p[:, 2 * D_POOL + 2 * D_CONV:D_IN]
    return u, g_pool, v, g_conv


def _prompt_tile(b, j, last_j, n_streams, may_start, may_end, x_ref, w, y_ref, sp_ref, sc_ref,
                 uext_ref, vext_ref, pk_ref, pooled_ref, act_ref, cat_ref):
    tt = TIME_TILE

    if may_start:
        @pl.when(j == 0)
        def _():
            _store_blocked(uext_ref, 0, jnp.zeros((POOL_PAD, D_POOL), F32))
            _store_blocked(vext_ref, 0, jnp.zeros((CONV_PAD, D_CONV), F32))
            _store_blocked(vext_ref, CONV_PAD + tt, jnp.zeros((SUBLANES, D_CONV), F32))

    x = x_ref[...]
    u, g_pool, v, g_conv = _project(x, w.ng, w.win)
    _store_blocked(uext_ref, POOL_PAD, u)
    _store_blocked(vext_ref, CONV_PAD, v)

    row = lax.broadcasted_iota(jnp.int32, (ROW_CHUNK, LANES), 0)

    def first_cnt(win):
        return jnp.where(j == 0, jnp.minimum(row + 1, win), win).astype(F32)

    for r0 in range(0, tt, ROW_CHUNK):
        pooled = _pool_rows(uext_ref, POOL_PAD + r0, ROW_CHUNK, first_cnt if r0 == 0 and may_start else None)
        pooled_ref[r0:r0 + ROW_CHUNK, :] = pooled.astype(BF16)
        lo = 0 if r0 == 0 else CONV_PAD + r0
        _pack_rows(vext_ref, pk_ref, lo, CONV_PAD + r0 + ROW_CHUNK - lo)
        c = _conv_rows(pk_ref, CONV_PAD + r0, ROW_CHUNK, w.dwu, w.dwb)
        act_ref[r0:r0 + ROW_CHUNK, :] = _conv_post(c, w.lng[...], w.lnb[...]).astype(BF16)

    y_ref[...] = _mix_and_project(x, g_pool, g_conv, pooled_ref, act_ref, cat_ref, w.bd, w.ps, w.pww, w.pwb,
                                  w.wout, w.fg)

    for bb in range(n_streams if may_end else 0):
        @pl.when((j == last_j) & (b == bb))
        def _(bb=bb):
            sp_ref[:, bb, :] = _load_blocked(uext_ref, POOL_PAD + tt - POOL_HIST, POOL_HIST)
            sc_ref[:, bb, :] = _load_blocked(vext_ref, CONV_PAD + tt - CONV_HIST, CONV_HIST)

    _store_blocked(uext_ref, 0, _load_blocked(uext_ref, tt, POOL_PAD))
    _store_blocked(vext_ref, 0, _load_blocked(vext_ref, tt, CONV_PAD))


def _sample_rows(n_streams, seq, x_ref, cp_ref, cc_ref, w, await_weight, y_ref, sp_ref, sc_ref,
                 uext_ref, vext_ref, pk_ref, pooled_ref, act_ref, cat_ref):
    x = x_ref[...]
    u, g_pool, v, g_conv = _project_by_rows(x, w.ng, w.win, await_weight)

    for s in range(n_streams):
        rows = slice(s * seq, (s + 1) * seq)
        _store_blocked(uext_ref, POOL_PAD - POOL_HIST, cp_ref[:, s, :])
        _store_blocked(uext_ref, POOL_PAD, u[rows, :])
        _store_blocked(vext_ref, CONV_PAD - CONV_HIST, cc_ref[:, s, :])
        _store_blocked(vext_ref, CONV_PAD, v[rows, :])
        pooled_ref[rows, :] = _pool_rows(uext_ref, POOL_PAD, seq).astype(BF16)
        _store_blocked(vext_ref, CONV_PAD + seq, jnp.zeros((SUBLANES, D_CONV), F32))
        _pack_rows(vext_ref, pk_ref, CONV_PAD - CONV_HIST, CONV_HIST + seq)
        c = _conv_rows(pk_ref, CONV_PAD, seq, w.dwu, w.dwb)
        act_ref[rows, :] = _conv_post(c, w.lng[...], w.lnb[...]).astype(BF16)
        sp_ref[:, s, :] = _load_blocked(uext_ref, POOL_PAD + seq - POOL_HIST, POOL_HIST)
        sc_ref[:, s, :] = _load_blocked(vext_ref, CONV_PAD + seq - CONV_HIST, CONV_HIST)

    y_ref[...] = _mix_and_project(x, g_pool, g_conv, pooled_ref.at[0:n_streams * seq], act_ref.at[0:n_streams * seq],
                                  cat_ref.at[0:n_streams * seq], w.bd, w.ps, w.pww, w.pwb, w.wout, w.fg,
                                  await_weight)


class _Weights:
    def __init__(self, ng, win, ps, dwb, lng, lnb, pww, pwb, wout, fg, bd, dwu):
        self.ng, self.win, self.ps, self.dwb, self.lng, self.lnb = ng, win, ps, dwb, lng, lnb
        self.pww, self.pwb, self.wout, self.fg, self.bd, self.dwu = pww, pwb, wout, fg, bd, dwu


def _weight_copies(win_hbm, pww_hbm, wout_hbm, win_ref, pww_ref, wout_ref, sem):
    copies = []
    for piece in range(N_WIN_PIECES):
        rows = pl.ds(piece * WIN_PIECE_ROWS, WIN_PIECE_ROWS)
        copies.append(pltpu.make_async_copy(win_hbm.at[0, rows, :], win_ref.at[rows, :], sem.at[piece]))
    copies.append(pltpu.make_async_copy(pww_hbm.at[0], pww_ref, sem.at[N_WIN_PIECES]))
    copies.append(pltpu.make_async_copy(wout_hbm.at[0], wout_ref, sem.at[N_WIN_PIECES + 1]))
    assert len(copies) == N_WEIGHT_COPIES
    return copies


def _mixer_kernel(xp_ref, xs_ref, cp_ref, cc_ref, ng_ref, win_hbm, pmix_ref, ps_ref, dww_ref, dwb_ref,
                  lng_ref, lnb_ref, pww_hbm, pwb_ref, wout_hbm, fg_ref,
                  yp_ref, ys_ref, spp_ref, scp_ref, sps_ref, scs_ref,
                  win_ref, pww_ref, wout_ref, copy_sem,
                  bd_ref, dwu_ref, uext_ref, vext_ref, pk_ref, pooled_ref, act_ref, cat_ref,
                  *, tiles_per_stream, n_prompt_streams, n_sample_streams, sample_seq):
    s = pl.program_id(0)
    w = _Weights(ng_ref, win_ref, ps_ref, dwb_ref, lng_ref, lnb_ref, pww_ref, pwb_ref, wout_ref, fg_ref,
                 bd_ref, dwu_ref)

    @pl.when(s == 0)
    def _():
        copies = _weight_copies(win_hbm, pww_hbm, wout_hbm, win_ref, pww_ref, wout_ref, copy_sem)
        for copy in copies:
            copy.start()
        _prepare_weights(pmix_ref, dww_ref, bd_ref, dwu_ref)
        _sample_rows(n_sample_streams, sample_seq, xs_ref, cp_ref, cc_ref, w, lambda i: copies[i].wait(),
                     ys_ref, sps_ref, scs_ref, uext_ref, vext_ref, pk_ref, pooled_ref, act_ref, cat_ref)

    @pl.when(s > 0)
    def _():
        steps_per_stream = tiles_per_stream // TILES_PER_STEP
        t = s - 1

        for i in range(TILES_PER_STEP):
            rows = pl.ds(i * TIME_TILE, TIME_TILE)
            _prompt_tile(t // steps_per_stream, lax.rem(t, steps_per_stream) * TILES_PER_STEP + i,
                         tiles_per_stream - 1, n_prompt_streams, i == 0, i == TILES_PER_STEP - 1,
                         xp_ref.at[rows], w, yp_ref.at[rows],
                         spp_ref, scp_ref, uext_ref, vext_ref, pk_ref, pooled_ref, act_ref, cat_ref)


def _resident(shape):
    return pl.BlockSpec(shape, lambda s: (0,) * len(shape), pipeline_mode=pl.Buffered(1))


def _whole(shape):
    return pl.BlockSpec(shape, lambda s: (0,) * len(shape))


def kernel(x_prompt, x_sample, cache_pool, cache_conv, norm_g, w_in, pool_mix, pool_scale, dw_w, dw_b,
           ln_g, ln_b, pw_w, pw_b, w_out, final_g):
    batch, seq, d_model = x_prompt.shape
    dec_batch, dec_seq, _ = x_sample.shape
    assert d_model == D_MODEL and w_in.shape == (1, D_MODEL, D_IN)
    assert pw_w.shape == (1, D_CONV, D_CONV) and w_out.shape == (1, D_MODEL, D_MODEL)
    step_rows = TILES_PER_STEP * TIME_TILE
    assert seq % step_rows == 0 and dec_seq % BF16_ROWS == 0 and dec_batch * dec_seq <= TIME_TILE
    tiles_per_stream = seq // TIME_TILE
    steps_per_stream = seq // step_rows
    n_steps = batch * steps_per_stream
    rows = dec_batch * dec_seq

    def step_index(s):
        t = jnp.maximum(s - 1, 0)
        return (t // steps_per_stream, t % steps_per_stream, 0)

    in_hbm = pl.BlockSpec(memory_space=pl.ANY)

    y_prompt, y_sample, sp_prompt, sc_prompt, sp_sample, sc_sample = pl.pallas_call(
        functools.partial(_mixer_kernel, tiles_per_stream=tiles_per_stream,
                          n_prompt_streams=batch, n_sample_streams=dec_batch, sample_seq=dec_seq),
        grid=(1 + n_steps,),
        in_specs=[
            pl.BlockSpec((None, step_rows, D_MODEL), step_index),
            _whole((rows, D_MODEL)),
            _whole((POOL_HIST, dec_batch, D_POOL)),
            _whole((CONV_HIST, dec_batch, D_CONV)),
            _resident((1, D_MODEL)),
            in_hbm,
            _resident((len(POOL_WINDOWS), POOL_GROUP, POOL_GROUP)),
            _resident((1, D_POOL)),
            _resident((CONV_WIDTH, D_CONV)),
            _resident((1, D_CONV)),
            _resident((1, D_CONV)),
            _resident((1, D_CONV)),
            in_hbm,
            _resident((1, D_CONV)),
            in_hbm,
            _resident((1, D_MODEL)),
        ],
        out_specs=[
            pl.BlockSpec((None, step_rows, D_MODEL), step_index),
            _whole((rows, D_MODEL)),
            _whole((POOL_HIST, batch, D_POOL)),
            _whole((CONV_HIST, batch, D_CONV)),
            _whole((POOL_HIST, dec_batch, D_POOL)),
            _whole((CONV_HIST, dec_batch, D_CONV)),
        ],
        out_shape=[
            jax.ShapeDtypeStruct((batch, seq, D_MODEL), F32),
            jax.ShapeDtypeStruct((rows, D_MODEL), F32),
            jax.ShapeDtypeStruct((POOL_HIST, batch, D_POOL), F32),
            jax.ShapeDtypeStruct((CONV_HIST, batch, D_CONV), F32),
            jax.ShapeDtypeStruct((POOL_HIST, dec_batch, D_POOL), F32),
            jax.ShapeDtypeStruct((CONV_HIST, dec_batch, D_CONV), F32),
        ],
        scratch_shapes=[
            pltpu.VMEM((D_MODEL, D_IN), F32),
            pltpu.VMEM((D_CONV, D_CONV), F32),
            pltpu.VMEM((D_MODEL, D_MODEL), F32),
            pltpu.SemaphoreType.DMA((N_WEIGHT_COPIES,)),
            pltpu.VMEM((D_POOL // (2 * POOL_GROUP), 2 * POOL_GROUP, 2 * POOL_GROUP), BF16),
            pltpu.VMEM((CONV_WIDTH, D_CONV), U32),
            pltpu.VMEM((D_POOL // LANES, POOL_PAD + TIME_TILE, LANES), F32),
            pltpu.VMEM((D_CONV // LANES, CONV_PAD + TIME_TILE + SUBLANES, LANES), F32),
            pltpu.VMEM((2, D_CONV // LANES, (CONV_PAD + TIME_TILE) // 2, LANES), U32),
            pltpu.VMEM((TIME_TILE, D_POOL), BF16),
            pltpu.VMEM((TIME_TILE, D_CONV), BF16),
            pltpu.VMEM((TIME_TILE, D_MODEL), BF16),
        ],
        compiler_params=pltpu.CompilerParams(
            dimension_semantics=("arbitrary",), vmem_limit_bytes=VMEM_LIMIT_BYTES),
        name="stream_mixer",
    )(x_prompt, x_sample.reshape(rows, D_MODEL), jnp.swapaxes(cache_pool[0], 0, 1),
      jnp.swapaxes(cache_conv[0], 0, 1), norm_g, w_in, pool_mix[0], pool_scale, dw_w[0], dw_b, ln_g, ln_b,
      pw_w, pw_b, w_out, final_g.reshape(1, D_MODEL))

    def as_state(t):
        return jnp.swapaxes(t, 0, 1)[None]

    return (y_prompt, y_sample.reshape(dec_batch, dec_seq, D_MODEL), as_state(sp_prompt), as_state(sc_prompt),
            as_state(sp_sample), as_state(sc_sample))
```

```python
import functools

import jax
import jax.numpy as jnp
from jax import lax
from jax.experimental import pallas as pl
from jax.experimental.pallas import tpu as pltpu

D_MODEL = 1024
D_POOL = 512
D_CONV = 512
POOL_WINDOWS = (2, 4, 8, 16)
POOL_GROUP = 128
POOL_HIST = 15
CONV_WIDTH = 31
CONV_HIST = 30
D_IN = 2 * D_POOL + 3 * D_CONV
RMS_EPS = 1e-6
LN_EPS = 1e-5

LANES = 128
SUBLANES = 8
BF16_ROWS = 2 * SUBLANES
POOL_PAD = 16
CONV_PAD = 32
TIME_TILE = 1024
TILES_PER_STEP = 1
ROW_CHUNK = 32
WIN_PIECE_ROWS = 256
N_WIN_PIECES = D_MODEL // WIN_PIECE_ROWS
N_WEIGHT_COPIES = N_WIN_PIECES + 2
VMEM_LIMIT_BYTES = 58 * 1024 * 1024

F32 = jnp.float32
BF16 = jnp.bfloat16
U32 = jnp.uint32


def _rms_norm(x, g):
    ms = jnp.mean(x * x, axis=-1, keepdims=True)
    return x * lax.rsqrt(ms + RMS_EPS) * g


def _silu(x):
    return x * jax.nn.sigmoid(x)


def _store_blocked(ext, row0, val):
    for cb in range(val.shape[-1] // LANES):
        ext[cb, pl.ds(row0, val.shape[0]), :] = val[:, cb * LANES:(cb + 1) * LANES]


def _load_blocked(ext, row0, rows):
    return jnp.concatenate([ext[cb, pl.ds(row0, rows), :] for cb in range(ext.shape[0])], axis=-1)


def _pool_rows(uext, row0, rows, cnt=None):
    outs = []
    for gi, w in enumerate(POOL_WINDOWS):
        cur = uext[gi, pl.ds(row0, rows), :]
        span = min(w, SUBLANES)
        acc = cur
        for i in range(1, span):
            acc = acc + uext[gi, pl.ds(row0 - i, rows), :]
        if w > span:
            assert w == 2 * span
            head = uext[gi, pl.ds(row0 - span, span), :]
            for i in range(1, span):
                head = head + uext[gi, pl.ds(row0 - span - i, span), :]
            acc = acc + jnp.concatenate([head, acc[:rows - span]], axis=0)
        if cnt is None:
            outs.append(acc * (1.0 / w) - cur)
        else:
            outs.append(acc / cnt(w) - cur)
    return jnp.concatenate(outs, axis=-1)


def _pack_rows(vext, pk, row0, rows):
    for cb in range(vext.shape[0]):
        for odd in (0, 1):
            x = vext[cb, pl.ds(row0 + odd, rows), :].astype(BF16)
            pk[odd, cb, pl.ds(row0 // 2, rows // 2), :] = pltpu.bitcast(x, U32)


def _conv_rows(pk, row0, rows, dwu_ref, dwb_ref):
    assert CONV_HIST % 2 == 0 and row0 % 2 == 0 and rows % BF16_ROWS == 0
    outs = []
    for cb in range(D_CONV // LANES):
        sl = slice(cb * LANES, (cb + 1) * LANES)
        part = [None] * 4
        for k in range(CONV_WIDTH):
            odd = k % 2
            start = (row0 + k - CONV_HIST - odd) // 2
            x = pltpu.bitcast(pk[odd, cb, pl.ds(start, rows // 2), :], BF16)
            w = pltpu.bitcast(jnp.broadcast_to(dwu_ref[k:k + 1, sl], (rows // 2, LANES)), BF16)
            part[k % 4] = x * w if part[k % 4] is None else part[k % 4] + x * w
        acc = (part[0] + part[1]) + (part[2] + part[3])
        outs.append(acc.astype(F32) + dwb_ref[:, sl])
    return jnp.concatenate(outs, axis=-1)


def _conv_post(c, lng, lnb):
    mu = jnp.mean(c, axis=-1, keepdims=True)
    xc = c - mu
    var = jnp.mean(xc * xc, axis=-1, keepdims=True)
    return _silu(xc * lax.rsqrt(var + LN_EPS) * lng + lnb)


def _dot(a, w):
    return lax.dot_general(a, w, (((1,), (0,)), ((), ())), preferred_element_type=F32)


def _prepare_weights(pmix_ref, dww_ref, bd_ref, dwu_ref):
    half = 2 * POOL_GROUP
    bd_ref[...] = jnp.zeros(bd_ref.shape, bd_ref.dtype)
    for g in range(len(POOL_WINDOWS)):
        lo = (g % 2) * POOL_GROUP
        bd_ref[g // 2, lo:lo + POOL_GROUP, lo:lo + POOL_GROUP] = pmix_ref[g].astype(BF16)
    assert bd_ref.shape == (D_POOL // half, half, half)
    for k in range(CONV_WIDTH):
        tile = jnp.broadcast_to(dww_ref[k:k + 1, :], (BF16_ROWS, D_CONV)).astype(BF16)
        dwu_ref[k:k + 1, :] = pltpu.bitcast(tile, U32)[0:1, :]


def _no_wait(copy_index):
    del copy_index


def _mix_and_project(x, g_pool, g_conv, pooled_ref, act_ref, cat_ref, bd_ref, ps_ref, pww_ref, pwb_ref,
                     wout_ref, fg_ref, await_weight=_no_wait):
    half = 2 * POOL_GROUP
    mix = jnp.concatenate([_dot(pooled_ref[:, i * half:(i + 1) * half], bd_ref[i])
                           for i in range(D_POOL // half)], axis=-1)
    cat_ref[:, 0:D_POOL] = (_silu(g_pool) * (mix * ps_ref[...])).astype(BF16)
    await_weight(N_WIN_PIECES)
    cmix = _dot(act_ref[...], pww_ref[...]) + pwb_ref[...]
    cat_ref[:, D_POOL:D_MODEL] = (_silu(g_conv) * cmix).astype(BF16)
    await_weight(N_WIN_PIECES + 1)
    return _rms_norm(x + _dot(cat_ref[...], wout_ref[...]), fg_ref[...])


def _project(x, ng_ref, win_ref):
    h = _rms_norm(x, ng_ref[...]).astype(BF16)

    def proj(lo, hi):
        return _dot(h, win_ref[:, lo:hi])

    u = proj(0, D_POOL)
    g_pool = proj(D_POOL, 2 * D_POOL)
    v = proj(2 * D_POOL, 2 * D_POOL + D_CONV) * jax.nn.sigmoid(proj(2 * D_POOL + D_CONV, 2 * D_POOL + 2 * D_CONV))
    g_conv = proj(2 * D_POOL + 2 * D_CONV, D_IN)
    return u, g_pool, v, g_conv


def _project_by_rows(x, ng_ref, win_ref, await_weight):
    h = _rms_norm(x, ng_ref[...]).astype(BF16)
    p = None
    for piece in range(N_WIN_PIECES):
        await_weight(piece)
        rows = slice(piece * WIN_PIECE_ROWS, (piece + 1) * WIN_PIECE_ROWS)
        part = _dot(h[:, rows], win_ref[rows, :])
        p = part if p is None else p + part
    u = p[:, 0:D_POOL]
    g_pool = p[:, D_POOL:2 * D_POOL]
    v = p[:, 2 * D_POOL:2 * D_POOL + D_CONV] * jax.nn.sigmoid(p[:, 2 * D_POOL + D_CONV:2 * D_POOL + 2 * D_CONV])
    g_conv = p[:, 2 * D_POOL + 2 * D_CONV:D_IN]
    return u, g_pool, v, g_conv


def _prompt_tile(b, j, last_j, n_streams, may_start, may_end, x_ref, w, y_ref, sp_ref, sc_ref,
                 uext_ref, vext_ref, pk_ref, pooled_ref, act_ref, cat_ref):
    tt = TIME_TILE

    if may_start:
        @pl.when(j == 0)
        def _():
            _store_blocked(uext_ref, 0, jnp.zeros((POOL_PAD, D_POOL), F32))
            _store_blocked(vext_ref, 0, jnp.zeros((CONV_PAD, D_CONV), F32))
            _store_blocked(vext_ref, CONV_PAD + tt, jnp.zeros((SUBLANES, D_CONV), F32))

    x = x_ref[...]
    u, g_pool, v, g_conv = _project(x, w.ng, w.win)
    _store_blocked(uext_ref, POOL_PAD, u)
    _store_blocked(vext_ref, CONV_PAD, v)

    row = lax.broadcasted_iota(jnp.int32, (ROW_CHUNK, LANES), 0)

    def first_cnt(win):
        return jnp.where(j == 0, jnp.minimum(row + 1, win), win).astype(F32)

    for r0 in range(0, tt, ROW_CHUNK):
        pooled = _pool_rows(uext_ref, POOL_PAD + r0, ROW_CHUNK, first_cnt if r0 == 0 and may_start else None)
        pooled_ref[r0:r0 + ROW_CHUNK, :] = pooled.astype(BF16)
        lo = 0 if r0 == 0 else CONV_PAD + r0
        _pack_rows(vext_ref, pk_ref, lo, CONV_PAD + r0 + ROW_CHUNK - lo)
        c = _conv_rows(pk_ref, CONV_PAD + r0, ROW_CHUNK, w.dwu, w.dwb)
        act_ref[r0:r0 + ROW_CHUNK, :] = _conv_post(c, w.lng[...], w.lnb[...]).astype(BF16)

    y_ref[...] = _mix_and_project(x, g_pool, g_conv, pooled_ref, act_ref, cat_ref, w.bd, w.ps, w.pww, w.pwb,
                                  w.wout, w.fg)

    for bb in range(n_streams if may_end else 0):
        @pl.when((j == last_j) & (b == bb))
        def _(bb=bb):
            sp_ref[:, bb, :] = _load_blocked(uext_ref, POOL_PAD + tt - POOL_HIST, POOL_HIST)
            sc_ref[:, bb, :] = _load_blocked(vext_ref, CONV_PAD + tt - CONV_HIST, CONV_HIST)

    _store_blocked(uext_ref, 0, _load_blocked(uext_ref, tt, POOL_PAD))
    _store_blocked(vext_ref, 0, _load_blocked(vext_ref, tt, CONV_PAD))


def _sample_rows(n_streams, seq, x_ref, cp_ref, cc_ref, w, await_weight, y_ref, sp_ref, sc_ref,
                 uext_ref, vext_ref, pk_ref, pooled_ref, act_ref, cat_ref):
    x = x_ref[...]
    u, g_pool, v, g_conv = _project_by_rows(x, w.ng, w.win, await_weight)

    for s in range(n_streams):
        rows = slice(s * seq, (s + 1) * seq)
        _store_blocked(uext_ref, POOL_PAD - POOL_HIST, cp_ref[:, s, :])
        _store_blocked(uext_ref, POOL_PAD, u[rows, :])
        _store_blocked(vext_ref, CONV_PAD - CONV_HIST, cc_ref[:, s, :])
        _store_blocked(vext_ref, CONV_PAD, v[rows, :])
        pooled_ref[rows, :] = _pool_rows(uext_ref, POOL_PAD, seq).astype(BF16)
        _store_blocked(vext_ref, CONV_PAD + seq, jnp.zeros((SUBLANES, D_CONV), F32))
        _pack_rows(vext_ref, pk_ref, CONV_PAD - CONV_HIST, CONV_HIST + seq)
        c = _conv_rows(pk_ref, CONV_PAD, seq, w.dwu, w.dwb)
        act_ref[rows, :] = _conv_post(c, w.lng[...], w.lnb[...]).astype(BF16)
        sp_ref[:, s, :] = _load_blocked(uext_ref, POOL_PAD + seq - POOL_HIST, POOL_HIST)
        sc_ref[:, s, :] = _load_blocked(vext_ref, CONV_PAD + seq - CONV_HIST, CONV_HIST)

    y_ref[...] = _mix_and_project(x, g_pool, g_conv, pooled_ref.at[0:n_streams * seq], act_ref.at[0:n_streams * seq],
                                  cat_ref.at[0:n_streams * seq], w.bd, w.ps, w.pww, w.pwb, w.wout, w.fg,
                                  await_weight)


class _Weights:
    def __init__(self, ng, win, ps, dwb, lng, lnb, pww, pwb, wout, fg, bd, dwu):
        self.ng, self.win, self.ps, self.dwb, self.lng, self.lnb = ng, win, ps, dwb, lng, lnb
        self.pww, self.pwb, self.wout, self.fg, self.bd, self.dwu = pww, pwb, wout, fg, bd, dwu


def _weight_copies(win_hbm, pww_hbm, wout_hbm, win_ref, pww_ref, wout_ref, sem):
    copies = []
    for piece in range(N_WIN_PIECES):
        rows = pl.ds(piece * WIN_PIECE_ROWS, WIN_PIECE_ROWS)
        copies.append(pltpu.make_async_copy(win_hbm.at[0, rows, :], win_ref.at[rows, :], sem.at[piece]))
    copies.append(pltpu.make_async_copy(pww_hbm.at[0], pww_ref, sem.at[N_WIN_PIECES]))
    copies.append(pltpu.make_async_copy(wout_hbm.at[0], wout_ref, sem.at[N_WIN_PIECES + 1]))
    assert len(copies) == N_WEIGHT_COPIES
    return copies


def _mixer_kernel(xp_ref, xs_ref, cp_ref, cc_ref, ng_ref, win_hbm, pmix_ref, ps_ref, dww_ref, dwb_ref,
                  lng_ref, lnb_ref, pww_hbm, pwb_ref, wout_hbm, fg_ref,
                  yp_ref, ys_ref, spp_ref, scp_ref, sps_ref, scs_ref,
                  win_ref, pww_ref, wout_ref, copy_sem,
                  bd_ref, dwu_ref, uext_ref, vext_ref, pk_ref, pooled_ref, act_ref, cat_ref,
                  *, tiles_per_stream, n_prompt_streams, n_sample_streams, sample_seq):
    s = pl.program_id(0)
    w = _Weights(ng_ref, win_ref, ps_ref, dwb_ref, lng_ref, lnb_ref, pww_ref, pwb_ref, wout_ref, fg_ref,
                 bd_ref, dwu_ref)

    @pl.when(s == 0)
    def _():
        copies = _weight_copies(win_hbm, pww_hbm, wout_hbm, win_ref, pww_ref, wout_ref, copy_sem)
        for copy in copies:
            copy.start()
        _prepare_weights(pmix_ref, dww_ref, bd_ref, dwu_ref)
        _sample_rows(n_sample_streams, sample_seq, xs_ref, cp_ref, cc_ref, w, lambda i: copies[i].wait(),
                     ys_ref, sps_ref, scs_ref, uext_ref, vext_ref, pk_ref, pooled_ref, act_ref, cat_ref)

    @pl.when(s > 0)
    def _():
        steps_per_stream = tiles_per_stream // TILES_PER_STEP
        t = s - 1

        for i in range(TILES_PER_STEP):
            rows = pl.ds(i * TIME_TILE, TIME_TILE)
            _prompt_tile(t // steps_per_stream, lax.rem(t, steps_per_stream) * TILES_PER_STEP + i,
                         tiles_per_stream - 1, n_prompt_streams, i == 0, i == TILES_PER_STEP - 1,
                         xp_ref.at[rows], w, yp_ref.at[rows],
                         spp_ref, scp_ref, uext_ref, vext_ref, pk_ref, pooled_ref, act_ref, cat_ref)


def _resident(shape):
    return pl.BlockSpec(shape, lambda s: (0,) * len(shape), pipeline_mode=pl.Buffered(1))


def _whole(shape):
    return pl.BlockSpec(shape, lambda s: (0,) * len(shape))


def kernel(x_prompt, x_sample, cache_pool, cache_conv, norm_g, w_in, pool_mix, pool_scale, dw_w, dw_b,
           ln_g, ln_b, pw_w, pw_b, w_out, final_g):
    batch, seq, d_model = x_prompt.shape
    dec_batch, dec_seq, _ = x_sample.shape
    assert d_model == D_MODEL and w_in.shape == (1, D_MODEL, D_IN)
    assert pw_w.shape == (1, D_CONV, D_CONV) and w_out.shape == (1, D_MODEL, D_MODEL)
    step_rows = TILES_PER_STEP * TIME_TILE
    assert seq % step_rows == 0 and dec_seq % BF16_ROWS == 0 and dec_batch * dec_seq <= TIME_TILE
    tiles_per_stream = seq // TIME_TILE
    steps_per_stream = seq // step_rows
    n_steps = batch * steps_per_stream
    rows = dec_batch * dec_seq

    def step_index(s):
        t = jnp.maximum(s - 1, 0)
        return (t // steps_per_stream, t % steps_per_stream, 0)

    in_hbm = pl.BlockSpec(memory_space=pl.ANY)

    y_prompt, y_sample, sp_prompt, sc_prompt, sp_sample, sc_sample = pl.pallas_call(
        functools.partial(_mixer_kernel, tiles_per_stream=tiles_per_stream,
                          n_prompt_streams=batch, n_sample_streams=dec_batch, sample_seq=dec_seq),
        grid=(1 + n_steps,),
        in_specs=[
            pl.BlockSpec((None, step_rows, D_MODEL), step_index),
            _whole((rows, D_MODEL)),
            _whole((POOL_HIST, dec_batch, D_POOL)),
            _whole((CONV_HIST, dec_batch, D_CONV)),
            _resident((1, D_MODEL)),
            in_hbm,
            _resident((len(POOL_WINDOWS), POOL_GROUP, POOL_GROUP)),
            _resident((1, D_POOL)),
            _resident((CONV_WIDTH, D_CONV)),
            _resident((1, D_CONV)),
            _resident((1, D_CONV)),
            _resident((1, D_CONV)),
            in_hbm,
            _resident((1, D_CONV)),
            in_hbm,
            _resident((1, D_MODEL)),
        ],
        out_specs=[
            pl.BlockSpec((None, step_rows, D_MODEL), step_index),
            _whole((rows, D_MODEL)),
            _whole((POOL_HIST, batch, D_POOL)),
            _whole((CONV_HIST, batch, D_CONV)),
            _whole((POOL_HIST, dec_batch, D_POOL)),
            _whole((CONV_HIST, dec_batch, D_CONV)),
        ],
        out_shape=[
            jax.ShapeDtypeStruct((batch, seq, D_MODEL), F32),
            jax.ShapeDtypeStruct((rows, D_MODEL), F32),
            jax.ShapeDtypeStruct((POOL_HIST, batch, D_POOL), F32),
            jax.ShapeDtypeStruct((CONV_HIST, batch, D_CONV), F32),
            jax.ShapeDtypeStruct((POOL_HIST, dec_batch, D_POOL), F32),
            jax.ShapeDtypeStruct((CONV_HIST, dec_batch, D_CONV), F32),
        ],
        scratch_shapes=[
            pltpu.VMEM((D_MODEL, D_IN), F32),
            pltpu.VMEM((D_CONV, D_CONV), F32),
            pltpu.VMEM((D_MODEL, D_MODEL), F32),
            pltpu.SemaphoreType.DMA((N_WEIGHT_COPIES,)),
            pltpu.VMEM((D_POOL // (2 * POOL_GROUP), 2 * POOL_GROUP, 2 * POOL_GROUP), BF16),
            pltpu.VMEM((CONV_WIDTH, D_CONV), U32),
            pltpu.VMEM((D_POOL // LANES, POOL_PAD + TIME_TILE, LANES), F32),
            pltpu.VMEM((D_CONV // LANES, CONV_PAD + TIME_TILE + SUBLANES, LANES), F32),
            pltpu.VMEM((2, D_CONV // LANES, (CONV_PAD + TIME_TILE) // 2, LANES), U32),
            pltpu.VMEM((TIME_TILE, D_POOL), BF16),
            pltpu.VMEM((TIME_TILE, D_CONV), BF16),
            pltpu.VMEM((TIME_TILE, D_MODEL), BF16),
        ],
        compiler_params=pltpu.CompilerParams(
            dimension_semantics=("arbitrary",), vmem_limit_bytes=VMEM_LIMIT_BYTES),
        name="stream_mixer",
    )(x_prompt, x_sample.reshape(rows, D_MODEL), jnp.swapaxes(cache_pool[0], 0, 1),
      jnp.swapaxes(cache_conv[0], 0, 1), norm_g, w_in, pool_mix[0], pool_scale, dw_w[0], dw_b, ln_g, ln_b,
      pw_w, pw_b, w_out, final_g.reshape(1, D_MODEL))

    def as_state(t):
        return jnp.swapaxes(t, 0, 1)[None]

    return (y_prompt, y_sample.reshape(dec_batch, dec_seq, D_MODEL), as_state(sp_prompt), as_state(sc_prompt),
            as_state(sp_sample), as_state(sc_sample))
```

```python
import functools

import jax
import jax.numpy as jnp
from jax import lax
from jax.experimental import pallas as pl
from jax.experimental.pallas import tpu as pltpu

D_MODEL = 1024
D_POOL = 512
D_CONV = 512
POOL_WINDOWS = (2, 4, 8, 16)
POOL_GROUP = 128
POOL_HIST = 15
CONV_WIDTH = 31
CONV_HIST = 30
D_IN = 2 * D_POOL + 3 * D_CONV
RMS_EPS = 1e-6
LN_EPS = 1e-5

LANES = 128
SUBLANES = 8
BF16_ROWS = 2 * SUBLANES
POOL_PAD = 16
CONV_PAD = 32
TIME_TILE = 1024
TILES_PER_STEP = 1
OUT_PIECES = 4
HEAD_PIECES = 2
ROW_CHUNK = 64
WIN_PIECE_ROWS = 256
N_WIN_PIECES = D_MODEL // WIN_PIECE_ROWS
N_WEIGHT_COPIES = N_WIN_PIECES + 2
VMEM_LIMIT_BYTES = 58 * 1024 * 1024

F32 = jnp.float32
BF16 = jnp.bfloat16
U32 = jnp.uint32


def _rms_norm(x, g):
    ms = jnp.mean(x * x, axis=-1, keepdims=True)
    return x * lax.rsqrt(ms + RMS_EPS) * g


def _silu(x):
    return x * jax.nn.sigmoid(x)


def _store_blocked(ext, row0, val):
    for cb in range(val.shape[-1] // LANES):
        ext[cb, pl.ds(row0, val.shape[0]), :] = val[:, cb * LANES:(cb + 1) * LANES]


def _load_blocked(ext, row0, rows):
    return jnp.concatenate([ext[cb, pl.ds(row0, rows), :] for cb in range(ext.shape[0])], axis=-1)


def _pool_rows(uext, row0, rows, cnt=None):
    outs = []
    for gi, w in enumerate(POOL_WINDOWS):
        cur = uext[gi, pl.ds(row0, rows), :]
        span = min(w, SUBLANES)
        acc = cur
        for i in range(1, span):
            acc = acc + uext[gi, pl.ds(row0 - i, rows), :]
        if w > span:
            assert w == 2 * span
            head = uext[gi, pl.ds(row0 - span, span), :]
            for i in range(1, span):
                head = head + uext[gi, pl.ds(row0 - span - i, span), :]
            acc = acc + jnp.concatenate([head, acc[:rows - span]], axis=0)
        if cnt is None:
            outs.append(acc * (1.0 / w) - cur)
        else:
            outs.append(acc / cnt(w) - cur)
    return jnp.concatenate(outs, axis=-1)


def _pack_rows(vext, pk, row0, rows):
    for cb in range(vext.shape[0]):
        for odd in (0, 1):
            x = vext[cb, pl.ds(row0 + odd, rows), :].astype(BF16)
            pk[odd, cb, pl.ds(row0 // 2, rows // 2), :] = pltpu.bitcast(x, U32)


def _conv_rows(pk, row0, rows, dwu_ref, dwb_ref):
    assert CONV_HIST % 2 == 0 and row0 % 2 == 0 and rows % BF16_ROWS == 0
    outs = []
    for cb in range(D_CONV // LANES):
        sl = slice(cb * LANES, (cb + 1) * LANES)
        part = [None] * 4
        for k in range(CONV_WIDTH):
            odd = k % 2
            start = (row0 + k - CONV_HIST - odd) // 2
            x = pltpu.bitcast(pk[odd, cb, pl.ds(start, rows // 2), :], BF16)
            w = pltpu.bitcast(jnp.broadcast_to(dwu_ref[k:k + 1, sl], (rows // 2, LANES)), BF16)
            part[k % 4] = x * w if part[k % 4] is None else part[k % 4] + x * w
        acc = (part[0] + part[1]) + (part[2] + part[3])
        outs.append(acc.astype(F32) + dwb_ref[:, sl])
    return jnp.concatenate(outs, axis=-1)


def _conv_post(c, lng, lnb):
    mu = jnp.mean(c, axis=-1, keepdims=True)
    xc = c - mu
    var = jnp.mean(xc * xc, axis=-1, keepdims=True)
    return _silu(xc * lax.rsqrt(var + LN_EPS) * lng + lnb)


def _dot(a, w):
    return lax.dot_general(a, w, (((1,), (0,)), ((), ())), preferred_element_type=F32)


def _prepare_weights(pmix_ref, dww_ref, bd_ref, dwu_ref):
    half = 2 * POOL_GROUP
    bd_ref[...] = jnp.zeros(bd_ref.shape, bd_ref.dtype)
    for g in range(len(POOL_WINDOWS)):
        lo = (g % 2) * POOL_GROUP
        bd_ref[g // 2, lo:lo + POOL_GROUP, lo:lo + POOL_GROUP] = pmix_ref[g].astype(BF16)
    assert bd_ref.shape == (D_POOL // half, half, half)
    for k in range(CONV_WIDTH):
        tile = jnp.broadcast_to(dww_ref[k:k + 1, :], (BF16_ROWS, D_CONV)).astype(BF16)
        dwu_ref[k:k + 1, :] = pltpu.bitcast(tile, U32)[0:1, :]


def _no_wait(copy_index):
    del copy_index


def _mix_and_project(x, g_pool, g_conv, pooled_ref, act_ref, cat_ref, bd_ref, ps_ref, pww_ref, pwb_ref,
                     wout_ref, fg_ref, out_pieces=1, await_weight=_no_wait):
    half = 2 * POOL_GROUP
    t = x.shape[0]
    head_pieces = min(out_pieces, HEAD_PIECES)
    assert t % out_pieces == 0 and out_pieces % head_pieces == 0
    per_head = out_pieces // head_pieces
    outs = []
    for hp in range(head_pieces):
        hrows = slice(hp * (t // head_pieces), (hp + 1) * (t // head_pieces))
        mix = jnp.concatenate([_dot(pooled_ref[hrows, c * half:(c + 1) * half], bd_ref[c])
                               for c in range(D_POOL // half)], axis=-1)
        cat_ref[hrows, 0:D_POOL] = (_silu(g_pool[hrows, :]) * (mix * ps_ref[...])).astype(BF16)
        if hp == 0:
            await_weight(N_WIN_PIECES)
        cmix = _dot(act_ref[hrows, :], pww_ref[...]) + pwb_ref[...]
        cat_ref[hrows, D_POOL:D_MODEL] = (_silu(g_conv[hrows, :]) * cmix).astype(BF16)
        if hp == 0:
            await_weight(N_WIN_PIECES + 1)
        for i in range(hp * per_head, (hp + 1) * per_head):
            rows = slice(i * (t // out_pieces), (i + 1) * (t // out_pieces))
            outs.append(_rms_norm(x[rows, :] + _dot(cat_ref[rows, :], wout_ref[...]), fg_ref[...]))
    return jnp.concatenate(outs, axis=0)


def _project(x, ng_ref, win_ref):
    h = _rms_norm(x, ng_ref[...]).astype(BF16)

    def proj(lo, hi):
        return _dot(h, win_ref[:, lo:hi])

    u = proj(0, D_POOL)
    g_pool = proj(D_POOL, 2 * D_POOL)
    v = proj(2 * D_POOL, 2 * D_POOL + D_CONV) * jax.nn.sigmoid(proj(2 * D_POOL + D_CONV, 2 * D_POOL + 2 * D_CONV))
    g_conv = proj(2 * D_POOL + 2 * D_CONV, D_IN)
    return u, g_pool, v, g_conv


def _project_by_rows(x, ng_ref, win_ref, await_weight):
    h = _rms_norm(x, ng_ref[...]).astype(BF16)
    p = None
    for piece in range(N_WIN_PIECES):
        await_weight(piece)
        rows = slice(piece * WIN_PIECE_ROWS, (piece + 1) * WIN_PIECE_ROWS)
        part = _dot(h[:, rows], win_ref[rows, :])
        p = part if p is None else p + part
    u = p[:, 0:D_POOL]
    g_pool = p[:, D_POOL:2 * D_POOL]
    v = p[:, 2 * D_POOL:2 * D_POOL + D_CONV] * jax.nn.sigmoid(p[:, 2 * D_POOL + D_CONV:2 * D_POOL + 2 * D_CONV])
    g_conv = p[:, 2 * D_POOL + 2 * D_CONV:D_IN]
    return u, g_pool, v, g_conv


def _prompt_tile(b, j, last_j, n_streams, may_start, may_end, x_ref, w, y_ref, sp_ref, sc_ref,
                 uext_ref, vext_ref, pk_ref, pooled_ref, act_ref, cat_ref):
    tt = TIME_TILE

    if may_start:
        @pl.when(j == 0)
        def _():
            _store_blocked(uext_ref, 0, jnp.zeros((POOL_PAD, D_POOL), F32))
            _store_blocked(vext_ref, 0, jnp.zeros((CONV_PAD, D_CONV), F32))
            _store_blocked(vext_ref, CONV_PAD + tt, jnp.zeros((SUBLANES, D_CONV), F32))

    x = x_ref[...]
    u, g_pool, v, g_conv = _project(x, w.ng, w.win)
    _store_blocked(uext_ref, POOL_PAD, u)
    _store_blocked(vext_ref, CONV_PAD, v)

    row = lax.broadcasted_iota(jnp.int32, (ROW_CHUNK, LANES), 0)

    def first_cnt(win):
        return jnp.where(j == 0, jnp.minimum(row + 1, win), win).astype(F32)

    for r0 in range(0, tt, ROW_CHUNK):
        pooled = _pool_rows(uext_ref, POOL_PAD + r0, ROW_CHUNK, first_cnt if r0 == 0 and may_start else None)
        pooled_ref[r0:r0 + ROW_CHUNK, :] = pooled.astype(BF16)
        lo = 0 if r0 == 0 else CONV_PAD + r0
        _pack_rows(vext_ref, pk_ref, lo, CONV_PAD + r0 + ROW_CHUNK - lo)
        c = _conv_rows(pk_ref, CONV_PAD + r0, ROW_CHUNK, w.dwu, w.dwb)
        act_ref[r0:r0 + ROW_CHUNK, :] = _conv_post(c, w.lng[...], w.lnb[...]).astype(BF16)

    y_ref[...] = _mix_and_project(x, g_pool, g_conv, pooled_ref, act_ref, cat_ref, w.bd, w.ps, w.pww, w.pwb,
                                  w.wout, w.fg, out_pieces=OUT_PIECES)

    for bb in range(n_streams if may_end else 0):
        @pl.when((j == last_j) & (b == bb))
        def _(bb=bb):
            sp_ref[:, bb, :] = _load_blocked(uext_ref, POOL_PAD + tt - POOL_HIST, POOL_HIST)
            sc_ref[:, bb, :] = _load_blocked(vext_ref, CONV_PAD + tt - CONV_HIST, CONV_HIST)

    _store_blocked(uext_ref, 0, _load_blocked(uext_ref, tt, POOL_PAD))
    _store_blocked(vext_ref, 0, _load_blocked(vext_ref, tt, CONV_PAD))


def _sample_rows(n_streams, seq, x_ref, cp_ref, cc_ref, w, await_weight, y_ref, sp_ref, sc_ref,
                 uext_ref, vext_ref, pk_ref, pooled_ref, act_ref, cat_ref):
    x = x_ref[...]
    u, g_pool, v, g_conv = _project_by_rows(x, w.ng, w.win, await_weight)

    for s in range(n_streams):
        rows = slice(s * seq, (s + 1) * seq)
        _store_blocked(uext_ref, POOL_PAD - POOL_HIST, cp_ref[:, s, :])
        _store_blocked(uext_ref, POOL_PAD, u[rows, :])
        _store_blocked(vext_ref, CONV_PAD - CONV_HIST, cc_ref[:, s, :])
        _store_blocked(vext_ref, CONV_PAD, v[rows, :])
        pooled_ref[rows, :] = _pool_rows(uext_ref, POOL_PAD, seq).astype(BF16)
        _store_blocked(vext_ref, CONV_PAD + seq, jnp.zeros((SUBLANES, D_CONV), F32))
        _pack_rows(vext_ref, pk_ref, CONV_PAD - CONV_HIST, CONV_HIST + seq)
        c = _conv_rows(pk_ref, CONV_PAD, seq, w.dwu, w.dwb)
        act_ref[rows, :] = _conv_post(c, w.lng[...], w.lnb[...]).astype(BF16)
        sp_ref[:, s, :] = _load_blocked(uext_ref, POOL_PAD + seq - POOL_HIST, POOL_HIST)
        sc_ref[:, s, :] = _load_blocked(vext_ref, CONV_PAD + seq - CONV_HIST, CONV_HIST)

    y_ref[...] = _mix_and_project(x, g_pool, g_conv, pooled_ref.at[0:n_streams * seq], act_ref.at[0:n_streams * seq],
                                  cat_ref.at[0:n_streams * seq], w.bd, w.ps, w.pww, w.pwb, w.wout, w.fg,
                                  await_weight=await_weight)


class _Weights:
    def __init__(self, ng, win, ps, dwb, lng, lnb, pww, pwb, wout, fg, bd, dwu):
        self.ng, self.win, self.ps, self.dwb, self.lng, self.lnb = ng, win, ps, dwb, lng, lnb
        self.pww, self.pwb, self.wout, self.fg, self.bd, self.dwu = pww, pwb, wout, fg, bd, dwu


def _weight_copies(win_hbm, pww_hbm, wout_hbm, win_ref, pww_ref, wout_ref, sem):
    copies = []
    for piece in range(N_WIN_PIECES):
        rows = pl.ds(piece * WIN_PIECE_ROWS, WIN_PIECE_ROWS)
        copies.append(pltpu.make_async_copy(win_hbm.at[0, rows, :], win_ref.at[rows, :], sem.at[piece]))
    copies.append(pltpu.make_async_copy(pww_hbm.at[0], pww_ref, sem.at[N_WIN_PIECES]))
    copies.append(pltpu.make_async_copy(wout_hbm.at[0], wout_ref, sem.at[N_WIN_PIECES + 1]))
    assert len(copies) == N_WEIGHT_COPIES
    return copies


def _mixer_kernel(xp_ref, xs_ref, cp_ref, cc_ref, ng_ref, win_hbm, pmix_ref, ps_ref, dww_ref, dwb_ref,
                  lng_ref, lnb_ref, pww_hbm, pwb_ref, wout_hbm, fg_ref,
                  yp_ref, ys_ref, spp_ref, scp_ref, sps_ref, scs_ref,
                  win_ref, pww_ref, wout_ref, copy_sem,
                  bd_ref, dwu_ref, uext_ref, vext_ref, pk_ref, pooled_ref, act_ref, cat_ref,
                  *, tiles_per_stream, n_prompt_streams, n_sample_streams, sample_seq):
    s = pl.program_id(0)
    w = _Weights(ng_ref, win_ref, ps_ref, dwb_ref, lng_ref, lnb_ref, pww_ref, pwb_ref, wout_ref, fg_ref,
                 bd_ref, dwu_ref)

    @pl.when(s == 0)
    def _():
        copies = _weight_copies(win_hbm, pww_hbm, wout_hbm, win_ref, pww_ref, wout_ref, copy_sem)
        for copy in copies:
            copy.start()
        _prepare_weights(pmix_ref, dww_ref, bd_ref, dwu_ref)
        _sample_rows(n_sample_streams, sample_seq, xs_ref, cp_ref, cc_ref, w, lambda i: copies[i].wait(),
                     ys_ref, sps_ref, scs_ref, uext_ref, vext_ref, pk_ref, pooled_ref, act_ref, cat_ref)

    @pl.when(s > 0)
    def _():
        steps_per_stream = tiles_per_stream // TILES_PER_STEP
        t = s - 1

        for i in range(TILES_PER_STEP):
            rows = pl.ds(i * TIME_TILE, TIME_TILE)
            _prompt_tile(t // steps_per_stream, lax.rem(t, steps_per_stream) * TILES_PER_STEP + i,
                         tiles_per_stream - 1, n_prompt_streams, i == 0, i == TILES_PER_STEP - 1,
                         xp_ref.at[rows], w, yp_ref.at[rows],
                         spp_ref, scp_ref, uext_ref, vext_ref, pk_ref, pooled_ref, act_ref, cat_ref)


def _resident(shape):
    return pl.BlockSpec(shape, lambda s: (0,) * len(shape), pipeline_mode=pl.Buffered(1))


def _whole(shape):
    return pl.BlockSpec(shape, lambda s: (0,) * len(shape))


def kernel(x_prompt, x_sample, cache_pool, cache_conv, norm_g, w_in, pool_mix, pool_scale, dw_w, dw_b,
           ln_g, ln_b, pw_w, pw_b, w_out, final_g):
    batch, seq, d_model = x_prompt.shape
    dec_batch, dec_seq, _ = x_sample.shape
    assert d_model == D_MODEL and w_in.shape == (1, D_MODEL, D_IN)
    assert pw_w.shape == (1, D_CONV, D_CONV) and w_out.shape == (1, D_MODEL, D_MODEL)
    step_rows = TILES_PER_STEP * TIME_TILE
    assert seq % step_rows == 0 and dec_seq % BF16_ROWS == 0 and dec_batch * dec_seq <= TIME_TILE
    tiles_per_stream = seq // TIME_TILE
    steps_per_stream = seq // step_rows
    n_steps = batch * steps_per_stream
    rows = dec_batch * dec_seq

    def step_index(s):
        t = jnp.maximum(s - 1, 0)
        return (t // steps_per_stream, t % steps_per_stream, 0)

    in_hbm = pl.BlockSpec(memory_space=pl.ANY)

    y_prompt, y_sample, sp_prompt, sc_prompt, sp_sample, sc_sample = pl.pallas_call(
        functools.partial(_mixer_kernel, tiles_per_stream=tiles_per_stream,
                          n_prompt_streams=batch, n_sample_streams=dec_batch, sample_seq=dec_seq),
        grid=(1 + n_steps,),
        in_specs=[
            pl.BlockSpec((None, step_rows, D_MODEL), step_index),
            _whole((rows, D_MODEL)),
            _whole((POOL_HIST, dec_batch, D_POOL)),
            _whole((CONV_HIST, dec_batch, D_CONV)),
            _resident((1, D_MODEL)),
            in_hbm,
            _resident((len(POOL_WINDOWS), POOL_GROUP, POOL_GROUP)),
            _resident((1, D_POOL)),
            _resident((CONV_WIDTH, D_CONV)),
            _resident((1, D_CONV)),
            _resident((1, D_CONV)),
            _resident((1, D_CONV)),
            in_hbm,
            _resident((1, D_CONV)),
            in_hbm,
            _resident((1, D_MODEL)),
        ],
        out_specs=[
            pl.BlockSpec((None, step_rows, D_MODEL), step_index),
            _whole((rows, D_MODEL)),
            _whole((POOL_HIST, batch, D_POOL)),
            _whole((CONV_HIST, batch, D_CONV)),
            _whole((POOL_HIST, dec_batch, D_POOL)),
            _whole((CONV_HIST, dec_batch, D_CONV)),
        ],
        out_shape=[
            jax.ShapeDtypeStruct((batch, seq, D_MODEL), F32),
            jax.ShapeDtypeStruct((rows, D_MODEL), F32),
            jax.ShapeDtypeStruct((POOL_HIST, batch, D_POOL), F32),
            jax.ShapeDtypeStruct((CONV_HIST, batch, D_CONV), F32),
            jax.ShapeDtypeStruct((POOL_HIST, dec_batch, D_POOL), F32),
            jax.ShapeDtypeStruct((CONV_HIST, dec_batch, D_CONV), F32),
        ],
        scratch_shapes=[
            pltpu.VMEM((D_MODEL, D_IN), F32),
            pltpu.VMEM((D_CONV, D_CONV), F32),
            pltpu.VMEM((D_MODEL, D_MODEL), F32),
            pltpu.SemaphoreType.DMA((N_WEIGHT_COPIES,)),
            pltpu.VMEM((D_POOL // (2 * POOL_GROUP), 2 * POOL_GROUP, 2 * POOL_GROUP), BF16),
            pltpu.VMEM((CONV_WIDTH, D_CONV), U32),
            pltpu.VMEM((D_POOL // LANES, POOL_PAD + TIME_TILE, LANES), F32),
            pltpu.VMEM((D_CONV // LANES, CONV_PAD + TIME_TILE + SUBLANES, LANES), F32),
            pltpu.VMEM((2, D_CONV // LANES, (CONV_PAD + TIME_TILE) // 2, LANES), U32),
            pltpu.VMEM((TIME_TILE, D_POOL), BF16),
            pltpu.VMEM((TIME_TILE, D_CONV), BF16),
            pltpu.VMEM((TIME_TILE, D_MODEL), BF16),
        ],
        compiler_params=pltpu.CompilerParams(
            dimension_semantics=("arbitrary",), vmem_limit_bytes=VMEM_LIMIT_BYTES),
        name="stream_mixer",
    )(x_prompt, x_sample.reshape(rows, D_MODEL), jnp.swapaxes(cache_pool[0], 0, 1),
      jnp.swapaxes(cache_conv[0], 0, 1), norm_g, w_in, pool_mix[0], pool_scale, dw_w[0], dw_b, ln_g, ln_b,
      pw_w, pw_b, w_out, final_g.reshape(1, D_MODEL))

    def as_state(t):
        return jnp.swapaxes(t, 0, 1)[None]

    return (y_prompt, y_sample.reshape(dec_batch, dec_seq, D_MODEL), as_state(sp_prompt), as_state(sc_prompt),
            as_state(sp_sample), as_state(sc_sample))
```

```python
import functools

import jax
import jax.numpy as jnp
from jax import lax
from jax.experimental import pallas as pl
from jax.experimental.pallas import tpu as pltpu

D_MODEL = 1024
D_POOL = 512
D_CONV = 512
POOL_WINDOWS = (2, 4, 8, 16)
POOL_GROUP = 128
POOL_HIST = 15
CONV_WIDTH = 31
CONV_HIST = 30
D_IN = 2 * D_POOL + 3 * D_CONV
RMS_EPS = 1e-6
LN_EPS = 1e-5

LANES = 128
SUBLANES = 8
BF16_ROWS = 2 * SUBLANES
POOL_PAD = 16
CONV_PAD = 32
TIME_TILE = 1024
TILES_PER_STEP = 1
OUT_PIECES = 4
HEAD_PIECES = 2
ROW_CHUNK = 64
WIN_PIECE_ROWS = 256
N_WIN_PIECES = D_MODEL // WIN_PIECE_ROWS
N_WEIGHT_COPIES = N_WIN_PIECES + 2
VMEM_LIMIT_BYTES = 58 * 1024 * 1024

F32 = jnp.float32
BF16 = jnp.bfloat16
U32 = jnp.uint32


def _rms_norm(x, g):
    ms = jnp.mean(x * x, axis=-1, keepdims=True)
    return x * lax.rsqrt(ms + RMS_EPS) * g


def _silu(x):
    return x * jax.nn.sigmoid(x)


def _store_blocked(ext, row0, val):
    for cb in range(val.shape[-1] // LANES):
        ext[cb, pl.ds(row0, val.shape[0]), :] = val[:, cb * LANES:(cb + 1) * LANES]


def _load_blocked(ext, row0, rows):
    return jnp.concatenate([ext[cb, pl.ds(row0, rows), :] for cb in range(ext.shape[0])], axis=-1)


def _pool_rows(uext, row0, rows, cnt=None):
    outs = []
    for gi, w in enumerate(POOL_WINDOWS):
        cur = uext[gi, pl.ds(row0, rows), :]
        span = min(w, SUBLANES)
        acc = cur
        for i in range(1, span):
            acc = acc + uext[gi, pl.ds(row0 - i, rows), :]
        if w > span:
            assert w == 2 * span
            head = uext[gi, pl.ds(row0 - span, span), :]
            for i in range(1, span):
                head = head + uext[gi, pl.ds(row0 - span - i, span), :]
            acc = acc + jnp.concatenate([head, acc[:rows - span]], axis=0)
        if cnt is None:
            outs.append(acc * (1.0 / w) - cur)
        else:
            outs.append(acc / cnt(w) - cur)
    return jnp.concatenate(outs, axis=-1)


def _pack_rows(vext, pk, row0, rows):
    for cb in range(vext.shape[0]):
        for odd in (0, 1):
            x = vext[cb, pl.ds(row0 + odd, rows), :].astype(BF16)
            pk[odd, cb, pl.ds(row0 // 2, rows // 2), :] = pltpu.bitcast(x, U32)


def _conv_rows(pk, row0, rows, dwu_ref, dwb_ref):
    assert CONV_HIST % 2 == 0 and row0 % 2 == 0 and rows % BF16_ROWS == 0
    outs = []
    for cb in range(D_CONV // LANES):
        sl = slice(cb * LANES, (cb + 1) * LANES)
        part = [None] * 4
        for k in range(CONV_WIDTH):
            odd = k % 2
            start = (row0 + k - CONV_HIST - odd) // 2
            x = pltpu.bitcast(pk[odd, cb, pl.ds(start, rows // 2), :], BF16)
            w = pltpu.bitcast(jnp.broadcast_to(dwu_ref[k:k + 1, sl], (rows // 2, LANES)), BF16)
            part[k % 4] = x * w if part[k % 4] is None else part[k % 4] + x * w
        acc = (part[0] + part[1]) + (part[2] + part[3])
        outs.append(acc.astype(F32) + dwb_ref[:, sl])
    return jnp.concatenate(outs, axis=-1)


def _conv_post(c, lng, lnb):
    mu = jnp.mean(c, axis=-1, keepdims=True)
    xc = c - mu
    var = jnp.mean(xc * xc, axis=-1, keepdims=True)
    return _silu(xc * lax.rsqrt(var + LN_EPS) * lng + lnb)


def _dot(a, w):
    return lax.dot_general(a, w, (((1,), (0,)), ((), ())), preferred_element_type=F32)


def _prepare_weights(pmix_ref, dww_ref, bd_ref, dwu_ref):
    half = 2 * POOL_GROUP
    bd_ref[...] = jnp.zeros(bd_ref.shape, bd_ref.dtype)
    for g in range(len(POOL_WINDOWS)):
        lo = (g % 2) * POOL_GROUP
        bd_ref[g // 2, lo:lo + POOL_GROUP, lo:lo + POOL_GROUP] = pmix_ref[g].astype(BF16)
    assert bd_ref.shape == (D_POOL // half, half, half)
    for k in range(CONV_WIDTH):
        tile = jnp.broadcast_to(dww_ref[k:k + 1, :], (BF16_ROWS, D_CONV)).astype(BF16)
        dwu_ref[k:k + 1, :] = pltpu.bitcast(tile, U32)[0:1, :]


def _no_wait(copy_index):
    del copy_index


def _mix_and_project(x, g_pool, g_conv, pooled_ref, act_ref, cat_ref, bd_ref, ps_ref, pww_ref, pwb_ref,
                     wout_ref, fg_ref, out_pieces=1, await_weight=_no_wait):
    half = 2 * POOL_GROUP
    t = x.shape[0]
    head_pieces = min(out_pieces, HEAD_PIECES)
    assert t % out_pieces == 0 and out_pieces % head_pieces == 0
    per_head = out_pieces // head_pieces
    outs = []
    for hp in range(head_pieces):
        hrows = slice(hp * (t // head_pieces), (hp + 1) * (t // head_pieces))
        mix = jnp.concatenate([_dot(pooled_ref[hrows, c * half:(c + 1) * half], bd_ref[c])
                               for c in range(D_POOL // half)], axis=-1)
        cat_ref[hrows, 0:D_POOL] = (_silu(g_pool[hrows, :]) * (mix * ps_ref[...])).astype(BF16)
        if hp == 0:
            await_weight(N_WIN_PIECES)
        cmix = _dot(act_ref[hrows, :], pww_ref[...]) + pwb_ref[...]
        cat_ref[hrows, D_POOL:D_MODEL] = (_silu(g_conv[hrows, :]) * cmix).astype(BF16)
        if hp == 0:
            await_weight(N_WIN_PIECES + 1)
        for i in range(hp * per_head, (hp + 1) * per_head):
            rows = slice(i * (t // out_pieces), (i + 1) * (t // out_pieces))
            outs.append(_rms_norm(x[rows, :] + _dot(cat_ref[rows, :], wout_ref[...]), fg_ref[...]))
    return jnp.concatenate(outs, axis=0)


def _dependent_zero(x):
    tiles = [x[r:r + BF16_ROWS, c:c + LANES] for r in range(0, x.shape[0], BF16_ROWS)
             for c in range(0, x.shape[1], LANES)]
    bits = pltpu.bitcast(functools.reduce(lambda p, q: p + q, tiles), U32)
    return pltpu.bitcast((bits >> 16) >> 16, BF16)


def _project(x, ng_ref, win_ref, h_ref, pool):
    h_ref[...] = _rms_norm(x, ng_ref[...]).astype(BF16)

    def proj(lo, hi):
        return _dot(h_ref[...], win_ref[:, lo:hi])

    done = pool(proj(0, D_POOL))
    v = proj(2 * D_POOL, 2 * D_POOL + D_CONV) * jax.nn.sigmoid(proj(2 * D_POOL + D_CONV, 2 * D_POOL + 2 * D_CONV))
    h_ref[0:BF16_ROWS, 0:LANES] = h_ref[0:BF16_ROWS, 0:LANES] + _dependent_zero(done)
    g_pool = proj(D_POOL, 2 * D_POOL)
    g_conv = proj(2 * D_POOL + 2 * D_CONV, D_IN)
    return g_pool, v, g_conv


def _project_by_rows(x, ng_ref, win_ref, await_weight):
    h = _rms_norm(x, ng_ref[...]).astype(BF16)
    p = None
    for piece in range(N_WIN_PIECES):
        await_weight(piece)
        rows = slice(piece * WIN_PIECE_ROWS, (piece + 1) * WIN_PIECE_ROWS)
        part = _dot(h[:, rows], win_ref[rows, :])
        p = part if p is None else p + part
    u = p[:, 0:D_POOL]
    g_pool = p[:, D_POOL:2 * D_POOL]
    v = p[:, 2 * D_POOL:2 * D_POOL + D_CONV] * jax.nn.sigmoid(p[:, 2 * D_POOL + D_CONV:2 * D_POOL + 2 * D_CONV])
    g_conv = p[:, 2 * D_POOL + 2 * D_CONV:D_IN]
    return u, g_pool, v, g_conv


def _prompt_tile(b, j, last_j, n_streams, may_start, may_end, x_ref, w, y_ref, sp_ref, sc_ref,
                 uext_ref, vext_ref, pk_ref, pooled_ref, act_ref, cat_ref):
    tt = TIME_TILE

    if may_start:
        @pl.when(j == 0)
        def _():
            _store_blocked(uext_ref, 0, jnp.zeros((POOL_PAD, D_POOL), F32))
            _store_blocked(vext_ref, 0, jnp.zeros((CONV_PAD, D_CONV), F32))
            _store_blocked(vext_ref, CONV_PAD + tt, jnp.zeros((SUBLANES, D_CONV), F32))

    row = lax.broadcasted_iota(jnp.int32, (ROW_CHUNK, LANES), 0)

    def first_cnt(win):
        return jnp.where(j == 0, jnp.minimum(row + 1, win), win).astype(F32)

    def pool(u):
        _store_blocked(uext_ref, POOL_PAD, u)
        pooled = []
        for r0 in range(0, tt, ROW_CHUNK):
            pooled.append(_pool_rows(uext_ref, POOL_PAD + r0, ROW_CHUNK,
                                     first_cnt if r0 == 0 and may_start else None).astype(BF16))
            pooled_ref[r0:r0 + ROW_CHUNK, :] = pooled[-1]
        return jnp.concatenate(pooled, axis=0)

    x = x_ref[...]
    g_pool, v, g_conv = _project(x, w.ng, w.win, cat_ref, pool)
    _store_blocked(vext_ref, CONV_PAD, v)

    for r0 in range(0, tt, ROW_CHUNK):
        lo = 0 if r0 == 0 else CONV_PAD + r0
        _pack_rows(vext_ref, pk_ref, lo, CONV_PAD + r0 + ROW_CHUNK - lo)
        c = _conv_rows(pk_ref, CONV_PAD + r0, ROW_CHUNK, w.dwu, w.dwb)
        act_ref[r0:r0 + ROW_CHUNK, :] = _conv_post(c, w.lng[...], w.lnb[...]).astype(BF16)

    y_ref[...] = _mix_and_project(x, g_pool, g_conv, pooled_ref, act_ref, cat_ref, w.bd, w.ps, w.pww, w.pwb,
                                  w.wout, w.fg, out_pieces=OUT_PIECES)

    for bb in range(n_streams if may_end else 0):
        @pl.when((j == last_j) & (b == bb))
        def _(bb=bb):
            sp_ref[:, bb, :] = _load_blocked(uext_ref, POOL_PAD + tt - POOL_HIST, POOL_HIST)
            sc_ref[:, bb, :] = _load_blocked(vext_ref, CONV_PAD + tt - CONV_HIST, CONV_HIST)

    _store_blocked(uext_ref, 0, _load_blocked(uext_ref, tt, POOL_PAD))
    _store_blocked(vext_ref, 0, _load_blocked(vext_ref, tt, CONV_PAD))


def _sample_rows(n_streams, seq, x_ref, cp_ref, cc_ref, w, await_weight, y_ref, sp_ref, sc_ref,
                 uext_ref, vext_ref, pk_ref, pooled_ref, act_ref, cat_ref):
    x = x_ref[...]
    u, g_pool, v, g_conv = _project_by_rows(x, w.ng, w.win, await_weight)

    for s in range(n_streams):
        rows = slice(s * seq, (s + 1) * seq)
        _store_blocked(uext_ref, POOL_PAD - POOL_HIST, cp_ref[:, s, :])
        _store_blocked(uext_ref, POOL_PAD, u[rows, :])
        _store_blocked(vext_ref, CONV_PAD - CONV_HIST, cc_ref[:, s, :])
        _store_blocked(vext_ref, CONV_PAD, v[rows, :])
        pooled_ref[rows, :] = _pool_rows(uext_ref, POOL_PAD, seq).astype(BF16)
        _store_blocked(vext_ref, CONV_PAD + seq, jnp.zeros((SUBLANES, D_CONV), F32))
        _pack_rows(vext_ref, pk_ref, CONV_PAD - CONV_HIST, CONV_HIST + seq)
        c = _conv_rows(pk_ref, CONV_PAD, seq, w.dwu, w.dwb)
        act_ref[rows, :] = _conv_post(c, w.lng[...], w.lnb[...]).astype(BF16)
        sp_ref[:, s, :] = _load_blocked(uext_ref, POOL_PAD + seq - POOL_HIST, POOL_HIST)
        sc_ref[:, s, :] = _load_blocked(vext_ref, CONV_PAD + seq - CONV_HIST, CONV_HIST)

    y_ref[...] = _mix_and_project(x, g_pool, g_conv, pooled_ref.at[0:n_streams * seq], act_ref.at[0:n_streams * seq],
                                  cat_ref.at[0:n_streams * seq], w.bd, w.ps, w.pww, w.pwb, w.wout, w.fg,
                                  await_weight=await_weight)


class _Weights:
    def __init__(self, ng, win, ps, dwb, lng, lnb, pww, pwb, wout, fg, bd, dwu):
        self.ng, self.win, self.ps, self.dwb, self.lng, self.lnb = ng, win, ps, dwb, lng, lnb
        self.pww, self.pwb, self.wout, self.fg, self.bd, self.dwu = pww, pwb, wout, fg, bd, dwu


def _weight_copies(win_hbm, pww_hbm, wout_hbm, win_ref, pww_ref, wout_ref, sem):
    copies = []
    for piece in range(N_WIN_PIECES):
        rows = pl.ds(piece * WIN_PIECE_ROWS, WIN_PIECE_ROWS)
        copies.append(pltpu.make_async_copy(win_hbm.at[0, rows, :], win_ref.at[rows, :], sem.at[piece]))
    copies.append(pltpu.make_async_copy(pww_hbm.at[0], pww_ref, sem.at[N_WIN_PIECES]))
    copies.append(pltpu.make_async_copy(wout_hbm.at[0], wout_ref, sem.at[N_WIN_PIECES + 1]))
    assert len(copies) == N_WEIGHT_COPIES
    return copies


def _mixer_kernel(xp_ref, xs_ref, cp_ref, cc_ref, ng_ref, win_hbm, pmix_ref, ps_ref, dww_ref, dwb_ref,
                  lng_ref, lnb_ref, pww_hbm, pwb_ref, wout_hbm, fg_ref,
                  yp_ref, ys_ref, spp_ref, scp_ref, sps_ref, scs_ref,
                  win_ref, pww_ref, wout_ref, copy_sem,
                  bd_ref, dwu_ref, uext_ref, vext_ref, pk_ref, pooled_ref, act_ref, cat_ref,
                  *, tiles_per_stream, n_prompt_streams, n_sample_streams, sample_seq):
    s = pl.program_id(0)
    w = _Weights(ng_ref, win_ref, ps_ref, dwb_ref, lng_ref, lnb_ref, pww_ref, pwb_ref, wout_ref, fg_ref,
                 bd_ref, dwu_ref)

    @pl.when(s == 0)
    def _():
        copies = _weight_copies(win_hbm, pww_hbm, wout_hbm, win_ref, pww_ref, wout_ref, copy_sem)
        for copy in copies:
            copy.start()
        _prepare_weights(pmix_ref, dww_ref, bd_ref, dwu_ref)
        _sample_rows(n_sample_streams, sample_seq, xs_ref, cp_ref, cc_ref, w, lambda i: copies[i].wait(),
                     ys_ref, sps_ref, scs_ref, uext_ref, vext_ref, pk_ref, pooled_ref, act_ref, cat_ref)

    @pl.when(s > 0)
    def _():
        steps_per_stream = tiles_per_stream // TILES_PER_STEP
        t = s - 1

        for i in range(TILES_PER_STEP):
            rows = pl.ds(i * TIME_TILE, TIME_TILE)
            _prompt_tile(t // steps_per_stream, lax.rem(t, steps_per_stream) * TILES_PER_STEP + i,
                         tiles_per_stream - 1, n_prompt_streams, i == 0, i == TILES_PER_STEP - 1,
                         xp_ref.at[rows], w, yp_ref.at[rows],
                         spp_ref, scp_ref, uext_ref, vext_ref, pk_ref, pooled_ref, act_ref, cat_ref)


def _resident(shape):
    return pl.BlockSpec(shape, lambda s: (0,) * len(shape), pipeline_mode=pl.Buffered(1))


def _whole(shape):
    return pl.BlockSpec(shape, lambda s: (0,) * len(shape))


def kernel(x_prompt, x_sample, cache_pool, cache_conv, norm_g, w_in, pool_mix, pool_scale, dw_w, dw_b,
           ln_g, ln_b, pw_w, pw_b, w_out, final_g):
    batch, seq, d_model = x_prompt.shape
    dec_batch, dec_seq, _ = x_sample.shape
    assert d_model == D_MODEL and w_in.shape == (1, D_MODEL, D_IN)
    assert pw_w.shape == (1, D_CONV, D_CONV) and w_out.shape == (1, D_MODEL, D_MODEL)
    step_rows = TILES_PER_STEP * TIME_TILE
    assert seq % step_rows == 0 and dec_seq % BF16_ROWS == 0 and dec_batch * dec_seq <= TIME_TILE
    tiles_per_stream = seq // TIME_TILE
    steps_per_stream = seq // step_rows
    n_steps = batch * steps_per_stream
    rows = dec_batch * dec_seq

    def step_index(s):
        t = jnp.maximum(s - 1, 0)
        return (t // steps_per_stream, t % steps_per_stream, 0)

    in_hbm = pl.BlockSpec(memory_space=pl.ANY)

    y_prompt, y_sample, sp_prompt, sc_prompt, sp_sample, sc_sample = pl.pallas_call(
        functools.partial(_mixer_kernel, tiles_per_stream=tiles_per_stream,
                          n_prompt_streams=batch, n_sample_streams=dec_batch, sample_seq=dec_seq),
        grid=(1 + n_steps,),
        in_specs=[
            pl.BlockSpec((None, step_rows, D_MODEL), step_index),
            _whole((rows, D_MODEL)),
            _whole((POOL_HIST, dec_batch, D_POOL)),
            _whole((CONV_HIST, dec_batch, D_CONV)),
            _resident((1, D_MODEL)),
            in_hbm,
            _resident((len(POOL_WINDOWS), POOL_GROUP, POOL_GROUP)),
            _resident((1, D_POOL)),
            _resident((CONV_WIDTH, D_CONV)),
            _resident((1, D_CONV)),
            _resident((1, D_CONV)),
            _resident((1, D_CONV)),
            in_hbm,
            _resident((1, D_CONV)),
            in_hbm,
            _resident((1, D_MODEL)),
        ],
        out_specs=[
            pl.BlockSpec((None, step_rows, D_MODEL), step_index),
            _whole((rows, D_MODEL)),
            _whole((POOL_HIST, batch, D_POOL)),
            _whole((CONV_HIST, batch, D_CONV)),
            _whole((POOL_HIST, dec_batch, D_POOL)),
            _whole((CONV_HIST, dec_batch, D_CONV)),
        ],
        out_shape=[
            jax.ShapeDtypeStruct((batch, seq, D_MODEL), F32),
            jax.ShapeDtypeStruct((rows, D_MODEL), F32),
            jax.ShapeDtypeStruct((POOL_HIST, batch, D_POOL), F32),
            jax.ShapeDtypeStruct((CONV_HIST, batch, D_CONV), F32),
            jax.ShapeDtypeStruct((POOL_HIST, dec_batch, D_POOL), F32),
            jax.ShapeDtypeStruct((CONV_HIST, dec_batch, D_CONV), F32),
        ],
        scratch_shapes=[
            pltpu.VMEM((D_MODEL, D_IN), F32),
            pltpu.VMEM((D_CONV, D_CONV), F32),
            pltpu.VMEM((D_MODEL, D_MODEL), F32),
            pltpu.SemaphoreType.DMA((N_WEIGHT_COPIES,)),
            pltpu.VMEM((D_POOL // (2 * POOL_GROUP), 2 * POOL_GROUP, 2 * POOL_GROUP), BF16),
            pltpu.VMEM((CONV_WIDTH, D_CONV), U32),
            pltpu.VMEM((D_POOL // LANES, POOL_PAD + TIME_TILE, LANES), F32),
            pltpu.VMEM((D_CONV // LANES, CONV_PAD + TIME_TILE + SUBLANES, LANES), F32),
            pltpu.VMEM((2, D_CONV // LANES, (CONV_PAD + TIME_TILE) // 2, LANES), U32),
            pltpu.VMEM((TIME_TILE, D_POOL), BF16),
            pltpu.VMEM((TIME_TILE, D_CONV), BF16),
            pltpu.VMEM((TIME_TILE, D_MODEL), BF16),
        ],
        compiler_params=pltpu.CompilerParams(
            dimension_semantics=("arbitrary",), vmem_limit_bytes=VMEM_LIMIT_BYTES),
        name="stream_mixer",
    )(x_prompt, x_sample.reshape(rows, D_MODEL), jnp.swapaxes(cache_pool[0], 0, 1),
      jnp.swapaxes(cache_conv[0], 0, 1), norm_g, w_in, pool_mix[0], pool_scale, dw_w[0], dw_b, ln_g, ln_b,
      pw_w, pw_b, w_out, final_g.reshape(1, D_MODEL))

    def as_state(t):
        return jnp.swapaxes(t, 0, 1)[None]

    return (y_prompt, y_sample.reshape(dec_batch, dec_seq, D_MODEL), as_state(sp_prompt), as_state(sc_prompt),
            as_state(sp_sample), as_state(sc_sample))
```

```python
import functools

import jax
import jax.numpy as jnp
from jax import lax
from jax.experimental import pallas as pl
from jax.experimental.pallas import tpu as pltpu

D_MODEL = 1024
D_POOL = 512
D_CONV = 512
POOL_WINDOWS = (2, 4, 8, 16)
POOL_GROUP = 128
POOL_HIST = 15
CONV_WIDTH = 31
CONV_HIST = 30
D_IN = 2 * D_POOL + 3 * D_CONV
RMS_EPS = 1e-6
LN_EPS = 1e-5

LANES = 128
SUBLANES = 8
BF16_ROWS = 2 * SUBLANES
POOL_PAD = 16
CONV_PAD = 32
TIME_TILE = 1024
TILES_PER_STEP = 1
OUT_PIECES = 4
HEAD_PIECES = 2
ROW_CHUNK = 64
WIN_PIECE_ROWS = 256
N_WIN_PIECES = D_MODEL // WIN_PIECE_ROWS
N_WEIGHT_COPIES = N_WIN_PIECES + 2
VMEM_LIMIT_BYTES = 58 * 1024 * 1024

F32 = jnp.float32
BF16 = jnp.bfloat16
U32 = jnp.uint32


def _rms_norm(x, g):
    ms = jnp.mean(x * x, axis=-1, keepdims=True)
    return x * lax.rsqrt(ms + RMS_EPS) * g


def _silu(x):
    return x * jax.nn.sigmoid(x)


def _store_blocked(ext, row0, val):
    for cb in range(val.shape[-1] // LANES):
        ext[cb, pl.ds(row0, val.shape[0]), :] = val[:, cb * LANES:(cb + 1) * LANES]


def _load_blocked(ext, row0, rows):
    return jnp.concatenate([ext[cb, pl.ds(row0, rows), :] for cb in range(ext.shape[0])], axis=-1)


def _pool_rows(uext, row0, rows, cnt=None):
    outs = []
    for gi, w in enumerate(POOL_WINDOWS):
        cur = uext[gi, pl.ds(row0, rows), :]
        span = min(w, SUBLANES)
        acc = cur
        for i in range(1, span):
            acc = acc + uext[gi, pl.ds(row0 - i, rows), :]
        if w > span:
            assert w == 2 * span
            head = uext[gi, pl.ds(row0 - span, span), :]
            for i in range(1, span):
                head = head + uext[gi, pl.ds(row0 - span - i, span), :]
            acc = acc + jnp.concatenate([head, acc[:rows - span]], axis=0)
        if cnt is None:
            outs.append(acc * (1.0 / w) - cur)
        else:
            outs.append(acc / cnt(w) - cur)
    return jnp.concatenate(outs, axis=-1)


def _pack_rows(vext, pk, row0, rows):
    for cb in range(vext.shape[0]):
        for odd in (0, 1):
            x = vext[cb, pl.ds(row0 + odd, rows), :].astype(BF16)
            pk[odd, cb, pl.ds(row0 // 2, rows // 2), :] = pltpu.bitcast(x, U32)


def _conv_rows(pk, row0, rows, dwu_ref, dwb_ref):
    assert CONV_HIST % 2 == 0 and row0 % 2 == 0 and rows % BF16_ROWS == 0
    outs = []
    for cb in range(D_CONV // LANES):
        sl = slice(cb * LANES, (cb + 1) * LANES)
        part = [None] * 4
        for k in range(CONV_WIDTH):
            odd = k % 2
            start = (row0 + k - CONV_HIST - odd) // 2
            x = pltpu.bitcast(pk[odd, cb, pl.ds(start, rows // 2), :], BF16)
            w = pltpu.bitcast(jnp.broadcast_to(dwu_ref[k:k + 1, sl], (rows // 2, LANES)), BF16)
            part[k % 4] = x * w if part[k % 4] is None else part[k % 4] + x * w
        acc = (part[0] + part[1]) + (part[2] + part[3])
        outs.append(acc.astype(F32) + dwb_ref[:, sl])
    return jnp.concatenate(outs, axis=-1)


def _conv_post(c, lng, lnb):
    mu = jnp.mean(c, axis=-1, keepdims=True)
    xc = c - mu
    var = jnp.mean(xc * xc, axis=-1, keepdims=True)
    return _silu(xc * lax.rsqrt(var + LN_EPS) * lng + lnb)


def _dot(a, w):
    return lax.dot_general(a, w, (((1,), (0,)), ((), ())), preferred_element_type=F32)


def _prepare_weights(pmix_ref, dww_ref, bd_ref, dwu_ref):
    half = 2 * POOL_GROUP
    bd_ref[...] = jnp.zeros(bd_ref.shape, bd_ref.dtype)
    for g in range(len(POOL_WINDOWS)):
        lo = (g % 2) * POOL_GROUP
        bd_ref[g // 2, lo:lo + POOL_GROUP, lo:lo + POOL_GROUP] = pmix_ref[g].astype(BF16)
    assert bd_ref.shape == (D_POOL // half, half, half)
    for k in range(CONV_WIDTH):
        tile = jnp.broadcast_to(dww_ref[k:k + 1, :], (BF16_ROWS, D_CONV)).astype(BF16)
        dwu_ref[k:k + 1, :] = pltpu.bitcast(tile, U32)[0:1, :]


def _no_wait(copy_index):
    del copy_index


def _mix_and_project(x_ref, g_pool, g_conv, pooled_ref, act_ref, cat_ref, bd_ref, ps_ref, pww_ref, pwb_ref,
                     wout_ref, fg_ref, out_pieces=1, await_weight=_no_wait):
    half = 2 * POOL_GROUP
    t = x_ref.shape[0]
    head_pieces = min(out_pieces, HEAD_PIECES)
    assert t % out_pieces == 0 and out_pieces % head_pieces == 0
    per_head = out_pieces // head_pieces
    outs = []
    for hp in range(head_pieces):
        hrows = slice(hp * (t // head_pieces), (hp + 1) * (t // head_pieces))
        mix = jnp.concatenate([_dot(pooled_ref[hrows, c * half:(c + 1) * half], bd_ref[c])
                               for c in range(D_POOL // half)], axis=-1)
        cat_ref[hrows, 0:D_POOL] = (_silu(g_pool[hrows, :]) * (mix * ps_ref[...])).astype(BF16)
        if hp == 0:
            await_weight(N_WIN_PIECES)
        cmix = _dot(act_ref[hrows, :], pww_ref[...]) + pwb_ref[...]
        cat_ref[hrows, D_POOL:D_MODEL] = (_silu(g_conv[hrows, :]) * cmix).astype(BF16)
        if hp == 0:
            await_weight(N_WIN_PIECES + 1)
        for i in range(hp * per_head, (hp + 1) * per_head):
            rows = slice(i * (t // out_pieces), (i + 1) * (t // out_pieces))
            outs.append(_rms_norm(x_ref[rows, :] + _dot(cat_ref[rows, :], wout_ref[...]), fg_ref[...]))
    return jnp.concatenate(outs, axis=0)


def _project(x, ng_ref, win_ref):
    h = _rms_norm(x, ng_ref[...]).astype(BF16)

    def proj(lo, hi):
        return _dot(h, win_ref[:, lo:hi])

    u = proj(0, D_POOL)
    g_pool = proj(D_POOL, 2 * D_POOL)
    v = proj(2 * D_POOL, 2 * D_POOL + D_CONV) * jax.nn.sigmoid(proj(2 * D_POOL + D_CONV, 2 * D_POOL + 2 * D_CONV))
    g_conv = proj(2 * D_POOL + 2 * D_CONV, D_IN)
    return u, g_pool, v, g_conv


def _project_by_rows(x, ng_ref, win_ref, await_weight):
    h = _rms_norm(x, ng_ref[...]).astype(BF16)
    p = None
    for piece in range(N_WIN_PIECES):
        await_weight(piece)
        rows = slice(piece * WIN_PIECE_ROWS, (piece + 1) * WIN_PIECE_ROWS)
        part = _dot(h[:, rows], win_ref[rows, :])
        p = part if p is None else p + part
    u = p[:, 0:D_POOL]
    g_pool = p[:, D_POOL:2 * D_POOL]
    v = p[:, 2 * D_POOL:2 * D_POOL + D_CONV] * jax.nn.sigmoid(p[:, 2 * D_POOL + D_CONV:2 * D_POOL + 2 * D_CONV])
    g_conv = p[:, 2 * D_POOL + 2 * D_CONV:D_IN]
    return u, g_pool, v, g_conv


def _prompt_tile(b, j, last_j, n_streams, may_start, may_end, x_ref, w, y_ref, sp_ref, sc_ref,
                 uext_ref, vext_ref, pk_ref, pooled_ref, act_ref, cat_ref):
    tt = TIME_TILE

    if may_start:
        @pl.when(j == 0)
        def _():
            _store_blocked(uext_ref, 0, jnp.zeros((POOL_PAD, D_POOL), F32))
            _store_blocked(vext_ref, 0, jnp.zeros((CONV_PAD, D_CONV), F32))
            _store_blocked(vext_ref, CONV_PAD + tt, jnp.zeros((SUBLANES, D_CONV), F32))

    u, g_pool, v, g_conv = _project(x_ref[...], w.ng, w.win)
    _store_blocked(uext_ref, POOL_PAD, u)
    _store_blocked(vext_ref, CONV_PAD, v)

    row = lax.broadcasted_iota(jnp.int32, (ROW_CHUNK, LANES), 0)

    def first_cnt(win):
        return jnp.where(j == 0, jnp.minimum(row + 1, win), win).astype(F32)

    for r0 in range(0, tt, ROW_CHUNK):
        pooled = _pool_rows(uext_ref, POOL_PAD + r0, ROW_CHUNK, first_cnt if r0 == 0 and may_start else None)
        pooled_ref[r0:r0 + ROW_CHUNK, :] = pooled.astype(BF16)
        lo = 0 if r0 == 0 else CONV_PAD + r0
        _pack_rows(vext_ref, pk_ref, lo, CONV_PAD + r0 + ROW_CHUNK - lo)
        c = _conv_rows(pk_ref, CONV_PAD + r0, ROW_CHUNK, w.dwu, w.dwb)
        act_ref[r0:r0 + ROW_CHUNK, :] = _conv_post(c, w.lng[...], w.lnb[...]).astype(BF16)

    y_ref[...] = _mix_and_project(x_ref, g_pool, g_conv, pooled_ref, act_ref, cat_ref, w.bd, w.ps, w.pww, w.pwb,
                                  w.wout, w.fg, out_pieces=OUT_PIECES)

    for bb in range(n_streams if may_end else 0):
        @pl.when((j == last_j) & (b == bb))
        def _(bb=bb):
            sp_ref[:, bb, :] = _load_blocked(uext_ref, POOL_PAD + tt - POOL_HIST, POOL_HIST)
            sc_ref[:, bb, :] = _load_blocked(vext_ref, CONV_PAD + tt - CONV_HIST, CONV_HIST)

    _store_blocked(uext_ref, 0, _load_blocked(uext_ref, tt, POOL_PAD))
    _store_blocked(vext_ref, 0, _load_blocked(vext_ref, tt, CONV_PAD))


def _sample_rows(n_streams, seq, x_ref, cp_ref, cc_ref, w, await_weight, y_ref, sp_ref, sc_ref,
                 uext_ref, vext_ref, pk_ref, pooled_ref, act_ref, cat_ref):
    u, g_pool, v, g_conv = _project_by_rows(x_ref[...], w.ng, w.win, await_weight)

    for s in range(n_streams):
        rows = slice(s * seq, (s + 1) * seq)
        _store_blocked(uext_ref, POOL_PAD - POOL_HIST, cp_ref[:, s, :])
        _store_blocked(uext_ref, POOL_PAD, u[rows, :])
        _store_blocked(vext_ref, CONV_PAD - CONV_HIST, cc_ref[:, s, :])
        _store_blocked(vext_ref, CONV_PAD, v[rows, :])
        pooled_ref[rows, :] = _pool_rows(uext_ref, POOL_PAD, seq).astype(BF16)
        _store_blocked(vext_ref, CONV_PAD + seq, jnp.zeros((SUBLANES, D_CONV), F32))
        _pack_rows(vext_ref, pk_ref, CONV_PAD - CONV_HIST, CONV_HIST + seq)
        c = _conv_rows(pk_ref, CONV_PAD, seq, w.dwu, w.dwb)
        act_ref[rows, :] = _conv_post(c, w.lng[...], w.lnb[...]).astype(BF16)
        sp_ref[:, s, :] = _load_blocked(uext_ref, POOL_PAD + seq - POOL_HIST, POOL_HIST)
        sc_ref[:, s, :] = _load_blocked(vext_ref, CONV_PAD + seq - CONV_HIST, CONV_HIST)

    y_ref[...] = _mix_and_project(x_ref, g_pool, g_conv, pooled_ref.at[0:n_streams * seq], act_ref.at[0:n_streams * seq],
                                  cat_ref.at[0:n_streams * seq], w.bd, w.ps, w.pww, w.pwb, w.wout, w.fg,
                                  await_weight=await_weight)


class _Weights:
    def __init__(self, ng, win, ps, dwb, lng, lnb, pww, pwb, wout, fg, bd, dwu):
        self.ng, self.win, self.ps, self.dwb, self.lng, self.lnb = ng, win, ps, dwb, lng, lnb
        self.pww, self.pwb, self.wout, self.fg, self.bd, self.dwu = pww, pwb, wout, fg, bd, dwu


def _weight_copies(win_hbm, pww_hbm, wout_hbm, win_ref, pww_ref, wout_ref, sem):
    copies = []
    for piece in range(N_WIN_PIECES):
        rows = pl.ds(piece * WIN_PIECE_ROWS, WIN_PIECE_ROWS)
        copies.append(pltpu.make_async_copy(win_hbm.at[0, rows, :], win_ref.at[rows, :], sem.at[piece]))
    copies.append(pltpu.make_async_copy(pww_hbm.at[0], pww_ref, sem.at[N_WIN_PIECES]))
    copies.append(pltpu.make_async_copy(wout_hbm.at[0], wout_ref, sem.at[N_WIN_PIECES + 1]))
    assert len(copies) == N_WEIGHT_COPIES
    return copies


def _mixer_kernel(xp_ref, xs_ref, cp_ref, cc_ref, ng_ref, win_hbm, pmix_ref, ps_ref, dww_ref, dwb_ref,
                  lng_ref, lnb_ref, pww_hbm, pwb_ref, wout_hbm, fg_ref,
                  yp_ref, ys_ref, spp_ref, scp_ref, sps_ref, scs_ref,
                  win_ref, pww_ref, wout_ref, copy_sem,
                  bd_ref, dwu_ref, uext_ref, vext_ref, pk_ref, pooled_ref, act_ref, cat_ref,
                  *, tiles_per_stream, n_prompt_streams, n_sample_streams, sample_seq):
    s = pl.program_id(0)
    w = _Weights(ng_ref, win_ref, ps_ref, dwb_ref, lng_ref, lnb_ref, pww_ref, pwb_ref, wout_ref, fg_ref,
                 bd_ref, dwu_ref)

    @pl.when(s == 0)
    def _():
        copies = _weight_copies(win_hbm, pww_hbm, wout_hbm, win_ref, pww_ref, wout_ref, copy_sem)
        for copy in copies:
            copy.start()
        _prepare_weights(pmix_ref, dww_ref, bd_ref, dwu_ref)
        _sample_rows(n_sample_streams, sample_seq, xs_ref, cp_ref, cc_ref, w, lambda i: copies[i].wait(),
                     ys_ref, sps_ref, scs_ref, uext_ref, vext_ref, pk_ref, pooled_ref, act_ref, cat_ref)

    @pl.when(s > 0)
    def _():
        steps_per_stream = tiles_per_stream // TILES_PER_STEP
        t = s - 1

        for i in range(TILES_PER_STEP):
            rows = pl.ds(i * TIME_TILE, TIME_TILE)
            _prompt_tile(t // steps_per_stream, lax.rem(t, steps_per_stream) * TILES_PER_STEP + i,
                         tiles_per_stream - 1, n_prompt_streams, i == 0, i == TILES_PER_STEP - 1,
                         xp_ref.at[rows], w, yp_ref.at[rows],
                         spp_ref, scp_ref, uext_ref, vext_ref, pk_ref, pooled_ref, act_ref, cat_ref)


def _resident(shape):
    return pl.BlockSpec(shape, lambda s: (0,) * len(shape), pipeline_mode=pl.Buffered(1))


def _whole(shape):
    return pl.BlockSpec(shape, lambda s: (0,) * len(shape))


def kernel(x_prompt, x_sample, cache_pool, cache_conv, norm_g, w_in, pool_mix, pool_scale, dw_w, dw_b,
           ln_g, ln_b, pw_w, pw_b, w_out, final_g):
    batch, seq, d_model = x_prompt.shape
    dec_batch, dec_seq, _ = x_sample.shape
    assert d_model == D_MODEL and w_in.shape == (1, D_MODEL, D_IN)
    assert pw_w.shape == (1, D_CONV, D_CONV) and w_out.shape == (1, D_MODEL, D_MODEL)
    step_rows = TILES_PER_STEP * TIME_TILE
    assert seq % step_rows == 0 and dec_seq % BF16_ROWS == 0 and dec_batch * dec_seq <= TIME_TILE
    tiles_per_stream = seq // TIME_TILE
    steps_per_stream = seq // step_rows
    n_steps = batch * steps_per_stream
    rows = dec_batch * dec_seq

    def step_index(s):
        t = jnp.maximum(s - 1, 0)
        return (t // steps_per_stream, t % steps_per_stream, 0)

    in_hbm = pl.BlockSpec(memory_space=pl.ANY)

    y_prompt, y_sample, sp_prompt, sc_prompt, sp_sample, sc_sample = pl.pallas_call(
        functools.partial(_mixer_kernel, tiles_per_stream=tiles_per_stream,
                          n_prompt_streams=batch, n_sample_streams=dec_batch, sample_seq=dec_seq),
        grid=(1 + n_steps,),
        in_specs=[
            pl.BlockSpec((None, step_rows, D_MODEL), step_index),
            _whole((rows, D_MODEL)),
            _whole((POOL_HIST, dec_batch, D_POOL)),
            _whole((CONV_HIST, dec_batch, D_CONV)),
            _resident((1, D_MODEL)),
            in_hbm,
            _resident((len(POOL_WINDOWS), POOL_GROUP, POOL_GROUP)),
            _resident((1, D_POOL)),
            _resident((CONV_WIDTH, D_CONV)),
            _resident((1, D_CONV)),
            _resident((1, D_CONV)),
            _resident((1, D_CONV)),
            in_hbm,
            _resident((1, D_CONV)),
            in_hbm,
            _resident((1, D_MODEL)),
        ],
        out_specs=[
            pl.BlockSpec((None, step_rows, D_MODEL), step_index),
            _whole((rows, D_MODEL)),
            _whole((POOL_HIST, batch, D_POOL)),
            _whole((CONV_HIST, batch, D_CONV)),
            _whole((POOL_HIST, dec_batch, D_POOL)),
            _whole((CONV_HIST, dec_batch, D_CONV)),
        ],
        out_shape=[
            jax.ShapeDtypeStruct((batch, seq, D_MODEL), F32),
            jax.ShapeDtypeStruct((rows, D_MODEL), F32),
            jax.ShapeDtypeStruct((POOL_HIST, batch, D_POOL), F32),
            jax.ShapeDtypeStruct((CONV_HIST, batch, D_CONV), F32),
            jax.ShapeDtypeStruct((POOL_HIST, dec_batch, D_POOL), F32),
            jax.ShapeDtypeStruct((CONV_HIST, dec_batch, D_CONV), F32),
        ],
        scratch_shapes=[
            pltpu.VMEM((D_MODEL, D_IN), F32),
            pltpu.VMEM((D_CONV, D_CONV), F32),
            pltpu.VMEM((D_MODEL, D_MODEL), F32),
            pltpu.SemaphoreType.DMA((N_WEIGHT_COPIES,)),
            pltpu.VMEM((D_POOL // (2 * POOL_GROUP), 2 * POOL_GROUP, 2 * POOL_GROUP), BF16),
            pltpu.VMEM((CONV_WIDTH, D_CONV), U32),
            pltpu.VMEM((D_POOL // LANES, POOL_PAD + TIME_TILE, LANES), F32),
            pltpu.VMEM((D_CONV // LANES, CONV_PAD + TIME_TILE + SUBLANES, LANES), F32),
            pltpu.VMEM((2, D_CONV // LANES, (CONV_PAD + TIME_TILE) // 2, LANES), U32),
            pltpu.VMEM((TIME_TILE, D_POOL), BF16),
            pltpu.VMEM((TIME_TILE, D_CONV), BF16),
            pltpu.VMEM((TIME_TILE, D_MODEL), BF16),
        ],
        compiler_params=pltpu.CompilerParams(
            dimension_semantics=("arbitrary",), vmem_limit_bytes=VMEM_LIMIT_BYTES),
        name="stream_mixer",
    )(x_prompt, x_sample.reshape(rows, D_MODEL), jnp.swapaxes(cache_pool[0], 0, 1),
      jnp.swapaxes(cache_conv[0], 0, 1), norm_g, w_in, pool_mix[0], pool_scale, dw_w[0], dw_b, ln_g, ln_b,
      pw_w, pw_b, w_out, final_g.reshape(1, D_MODEL))

    def as_state(t):
        return jnp.swapaxes(t, 0, 1)[None]

    return (y_prompt, y_sample.reshape(dec_batch, dec_seq, D_MODEL), as_state(sp_prompt), as_state(sc_prompt),
            as_state(sp_sample), as_state(sc_sample))
```

```python
import functools

import jax
import jax.numpy as jnp
from jax import lax
from jax.experimental import pallas as pl
from jax.experimental.pallas import tpu as pltpu

D_MODEL = 1024
D_POOL = 512
D_CONV = 512
POOL_WINDOWS = (2, 4, 8, 16)
POOL_GROUP = 128
POOL_HIST = 15
CONV_WIDTH = 31
CONV_HIST = 30
D_IN = 2 * D_POOL + 3 * D_CONV
RMS_EPS = 1e-6
LN_EPS = 1e-5

LANES = 128
SUBLANES = 8
BF16_ROWS = 2 * SUBLANES
POOL_PAD = 16
CONV_PAD = 32
TIME_TILE = 1024
TILES_PER_STEP = 1
OUT_PIECES = 4
HEAD_PIECES = 2
ROW_CHUNK = 64
WIN_PIECE_ROWS = 256
N_WIN_PIECES = D_MODEL // WIN_PIECE_ROWS
N_WEIGHT_COPIES = N_WIN_PIECES + 2
VMEM_LIMIT_BYTES = 58 * 1024 * 1024

F32 = jnp.float32
BF16 = jnp.bfloat16
U32 = jnp.uint32


def _rms_norm(x, g):
    ms = jnp.mean(x * x, axis=-1, keepdims=True)
    return x * lax.rsqrt(ms + RMS_EPS) * g


def _silu(x):
    return x * jax.nn.sigmoid(x)


def _store_blocked(ext, row0, val):
    for cb in range(val.shape[-1] // LANES):
        ext[cb, pl.ds(row0, val.shape[0]), :] = val[:, cb * LANES:(cb + 1) * LANES]


def _load_blocked(ext, row0, rows):
    return jnp.concatenate([ext[cb, pl.ds(row0, rows), :] for cb in range(ext.shape[0])], axis=-1)


def _pool_rows(uext, row0, rows, cnt=None):
    outs = []
    for gi, w in enumerate(POOL_WINDOWS):
        cur = uext[gi, pl.ds(row0, rows), :]
        span = min(w, SUBLANES)
        acc = cur
        for i in range(1, span):
            acc = acc + uext[gi, pl.ds(row0 - i, rows), :]
        if w > span:
            assert w == 2 * span
            head = uext[gi, pl.ds(row0 - span, span), :]
            for i in range(1, span):
                head = head + uext[gi, pl.ds(row0 - span - i, span), :]
            acc = acc + jnp.concatenate([head, acc[:rows - span]], axis=0)
        if cnt is None:
            outs.append(acc * (1.0 / w) - cur)
        else:
            outs.append(acc / cnt(w) - cur)
    return jnp.concatenate(outs, axis=-1)


def _pack_rows(vext, pk, row0, rows):
    for cb in range(vext.shape[0]):
        for odd in (0, 1):
            x = vext[cb, pl.ds(row0 + odd, rows), :].astype(BF16)
            pk[odd, cb, pl.ds(row0 // 2, rows // 2), :] = pltpu.bitcast(x, U32)


def _conv_rows(pk, row0, rows, dwu_ref, dwb_ref):
    assert CONV_HIST % 2 == 0 and row0 % 2 == 0 and rows % BF16_ROWS == 0
    outs = []
    for cb in range(D_CONV // LANES):
        sl = slice(cb * LANES, (cb + 1) * LANES)
        part = [None] * 4
        for k in range(CONV_WIDTH):
            odd = k % 2
            start = (row0 + k - CONV_HIST - odd) // 2
            x = pltpu.bitcast(pk[odd, cb, pl.ds(start, rows // 2), :], BF16)
            w = pltpu.bitcast(jnp.broadcast_to(dwu_ref[k:k + 1, sl], (rows // 2, LANES)), BF16)
            part[k % 4] = x * w if part[k % 4] is None else part[k % 4] + x * w
        acc = (part[0] + part[1]) + (part[2] + part[3])
        outs.append(acc.astype(F32) + dwb_ref[:, sl])
    return jnp.concatenate(outs, axis=-1)


def _conv_post(c, lng, lnb):
    mu = jnp.mean(c, axis=-1, keepdims=True)
    xc = c - mu
    var = jnp.mean(xc * xc, axis=-1, keepdims=True)
    return _silu(xc * lax.rsqrt(var + LN_EPS) * lng + lnb)


def _dot(a, w):
    return lax.dot_general(a, w, (((1,), (0,)), ((), ())), preferred_element_type=F32)


def _prepare_weights(pmix_ref, dww_ref, bd_ref, dwu_ref):
    half = 2 * POOL_GROUP
    bd_ref[...] = jnp.zeros(bd_ref.shape, bd_ref.dtype)
    for g in range(len(POOL_WINDOWS)):
        lo = (g % 2) * POOL_GROUP
        bd_ref[g // 2, lo:lo + POOL_GROUP, lo:lo + POOL_GROUP] = pmix_ref[g].astype(BF16)
    assert bd_ref.shape == (D_POOL // half, half, half)
    for k in range(CONV_WIDTH):
        tile = jnp.broadcast_to(dww_ref[k:k + 1, :], (BF16_ROWS, D_CONV)).astype(BF16)
        dwu_ref[k:k + 1, :] = pltpu.bitcast(tile, U32)[0:1, :]


def _no_wait(copy_index):
    del copy_index


def _mix_and_project(x_ref, g_pool, g_conv, pooled_ref, act_ref, cat_ref, bd_ref, ps_ref, pww_ref, pwb_ref,
                     wout_ref, fg_ref, out_pieces=1, await_weight=_no_wait):
    half = 2 * POOL_GROUP
    t = x_ref.shape[0]
    head_pieces = min(out_pieces, HEAD_PIECES)
    assert t % out_pieces == 0 and out_pieces % head_pieces == 0
    per_head = out_pieces // head_pieces
    outs = []
    for hp in range(head_pieces):
        hrows = slice(hp * (t // head_pieces), (hp + 1) * (t // head_pieces))
        mix = jnp.concatenate([_dot(pooled_ref[hrows, c * half:(c + 1) * half], bd_ref[c])
                               for c in range(D_POOL // half)], axis=-1)
        cat_ref[hrows, 0:D_POOL] = (_silu(g_pool[hrows, :]) * (mix * ps_ref[...])).astype(BF16)
        if hp == 0:
            await_weight(N_WIN_PIECES)
        cmix = _dot(act_ref[hrows, :], pww_ref[...]) + pwb_ref[...]
        cat_ref[hrows, D_POOL:D_MODEL] = (_silu(g_conv[hrows, :]) * cmix).astype(BF16)
        if hp == 0:
            await_weight(N_WIN_PIECES + 1)
        for i in range(hp * per_head, (hp + 1) * per_head):
            rows = slice(i * (t // out_pieces), (i + 1) * (t // out_pieces))
            outs.append(_rms_norm(x_ref[rows, :] + _dot(cat_ref[rows, :], wout_ref[...]), fg_ref[...]))
    return jnp.concatenate(outs, axis=0)


def _project(x, ng_ref, win_ref):
    h = _rms_norm(x, ng_ref[...]).astype(BF16)

    def proj(lo, hi):
        return _dot(h, win_ref[:, lo:hi])

    u = proj(0, D_POOL)
    g_pool = proj(D_POOL, 2 * D_POOL)
    v = proj(2 * D_POOL, 2 * D_POOL + D_CONV) * jax.nn.sigmoid(proj(2 * D_POOL + D_CONV, 2 * D_POOL + 2 * D_CONV))
    g_conv = proj(2 * D_POOL + 2 * D_CONV, D_IN)
    return u, g_pool, v, g_conv


def _project_by_rows(x, ng_ref, win_ref, await_weight):
    h = _rms_norm(x, ng_ref[...]).astype(BF16)
    p = None
    for piece in range(N_WIN_PIECES):
        await_weight(piece)
        rows = slice(piece * WIN_PIECE_ROWS, (piece + 1) * WIN_PIECE_ROWS)
        part = _dot(h[:, rows], win_ref[rows, :])
        p = part if p is None else p + part
    u = p[:, 0:D_POOL]
    g_pool = p[:, D_POOL:2 * D_POOL]
    v = p[:, 2 * D_POOL:2 * D_POOL + D_CONV] * jax.nn.sigmoid(p[:, 2 * D_POOL + D_CONV:2 * D_POOL + 2 * D_CONV])
    g_conv = p[:, 2 * D_POOL + 2 * D_CONV:D_IN]
    return u, g_pool, v, g_conv


def _prompt_tile(b, j, last_j, n_streams, may_start, may_end, x_ref, w, y_ref, sp_ref, sc_ref,
                 uext_ref, vext_ref, pk_ref, pooled_ref, act_ref, cat_ref):
    tt = TIME_TILE

    if may_start:
        @pl.when(j == 0)
        def _():
            _store_blocked(uext_ref, 0, jnp.zeros((POOL_PAD, D_POOL), F32))
            _store_blocked(vext_ref, 0, jnp.zeros((CONV_PAD, D_CONV), F32))
            _store_blocked(vext_ref, CONV_PAD + tt, jnp.zeros((SUBLANES, D_CONV), F32))

    u, g_pool, v, g_conv = _project(x_ref[...], w.ng, w.win)
    _store_blocked(uext_ref, POOL_PAD, u)
    _store_blocked(vext_ref, CONV_PAD, v)

    row = lax.broadcasted_iota(jnp.int32, (ROW_CHUNK, LANES), 0)

    def first_cnt(win):
        return jnp.where(j == 0, jnp.minimum(row + 1, win), win).astype(F32)

    for r0 in range(0, tt, ROW_CHUNK):
        pooled = _pool_rows(uext_ref, POOL_PAD + r0, ROW_CHUNK, first_cnt if r0 == 0 and may_start else None)
        pooled_ref[r0:r0 + ROW_CHUNK, :] = pooled.astype(BF16)
        lo = 0 if r0 == 0 else CONV_PAD + r0
        _pack_rows(vext_ref, pk_ref, lo, CONV_PAD + r0 + ROW_CHUNK - lo)
        c = _conv_rows(pk_ref, CONV_PAD + r0, ROW_CHUNK, w.dwu, w.dwb)
        act_ref[r0:r0 + ROW_CHUNK, :] = _conv_post(c, w.lng[...], w.lnb[...]).astype(BF16)

    y_ref[...] = _mix_and_project(x_ref, g_pool, g_conv, pooled_ref, act_ref, cat_ref, w.bd, w.ps, w.pww, w.pwb,
                                  w.wout, w.fg, out_pieces=OUT_PIECES)

    for bb in range(n_streams if may_end else 0):
        @pl.when((j == last_j) & (b == bb))
        def _(bb=bb):
            sp_ref[:, bb, :] = _load_blocked(uext_ref, POOL_PAD + tt - POOL_HIST, POOL_HIST)
            sc_ref[:, bb, :] = _load_blocked(vext_ref, CONV_PAD + tt - CONV_HIST, CONV_HIST)

    _store_blocked(uext_ref, 0, _load_blocked(uext_ref, tt, POOL_PAD))
    _store_blocked(vext_ref, 0, _load_blocked(vext_ref, tt, CONV_PAD))


def _sample_rows(n_streams, seq, x_ref, cp_ref, cc_ref, w, await_weight, y_ref, sp_ref, sc_ref,
                 uext_ref, vext_ref, pk_ref, pooled_ref, act_ref, cat_ref):
    u, g_pool, v, g_conv = _project_by_rows(x_ref[...], w.ng, w.win, await_weight)

    for s in range(n_streams):
        rows = slice(s * seq, (s + 1) * seq)
        _store_blocked(uext_ref, POOL_PAD - POOL_HIST, cp_ref[:, s, :])
        _store_blocked(uext_ref, POOL_PAD, u[rows, :])
        _store_blocked(vext_ref, CONV_PAD - CONV_HIST, cc_ref[:, s, :])
        _store_blocked(vext_ref, CONV_PAD, v[rows, :])
        pooled_ref[rows, :] = _pool_rows(uext_ref, POOL_PAD, seq).astype(BF16)
        _store_blocked(vext_ref, CONV_PAD + seq, jnp.zeros((SUBLANES, D_CONV), F32))
        _pack_rows(vext_ref, pk_ref, CONV_PAD - CONV_HIST, CONV_HIST + seq)
        c = _conv_rows(pk_ref, CONV_PAD, seq, w.dwu, w.dwb)
        act_ref[rows, :] = _conv_post(c, w.lng[...], w.lnb[...]).astype(BF16)
        sp_ref[:, s, :] = _load_blocked(uext_ref, POOL_PAD + seq - POOL_HIST, POOL_HIST)
        sc_ref[:, s, :] = _load_blocked(vext_ref, CONV_PAD + seq - CONV_HIST, CONV_HIST)

    y_ref[...] = _mix_and_project(x_ref, g_pool, g_conv, pooled_ref.at[0:n_streams * seq], act_ref.at[0:n_streams * seq],
                                  cat_ref.at[0:n_streams * seq], w.bd, w.ps, w.pww, w.pwb, w.wout, w.fg,
                                  await_weight=await_weight)


class _Weights:
    def __init__(self, ng, win, ps, dwb, lng, lnb, pww, pwb, wout, fg, bd, dwu):
        self.ng, self.win, self.ps, self.dwb, self.lng, self.lnb = ng, win, ps, dwb, lng, lnb
        self.pww, self.pwb, self.wout, self.fg, self.bd, self.dwu = pww, pwb, wout, fg, bd, dwu


def _start_weight_copies(win_hbm, pww_hbm, wout_hbm, win_ref, pww_ref, wout_ref, sem):
    copies = []
    for piece in range(N_WIN_PIECES):
        rows = pl.ds(piece * WIN_PIECE_ROWS, WIN_PIECE_ROWS)
        copies.append(pltpu.async_copy(win_hbm.at[0, rows, :], win_ref.at[rows, :], sem.at[piece], priority=0))
    copies.append(pltpu.async_copy(pww_hbm.at[0], pww_ref, sem.at[N_WIN_PIECES], priority=1))
    copies.append(pltpu.async_copy(wout_hbm.at[0], wout_ref, sem.at[N_WIN_PIECES + 1], priority=1))
    assert len(copies) == N_WEIGHT_COPIES
    return copies


def _mixer_kernel(xp_ref, xs_ref, cp_ref, cc_ref, ng_ref, win_hbm, pmix_ref, ps_ref, dww_ref, dwb_ref,
                  lng_ref, lnb_ref, pww_hbm, pwb_ref, wout_hbm, fg_ref,
                  yp_ref, ys_ref, spp_ref, scp_ref, sps_ref, scs_ref,
                  win_ref, pww_ref, wout_ref, copy_sem,
                  bd_ref, dwu_ref, uext_ref, vext_ref, pk_ref, pooled_ref, act_ref, cat_ref,
                  *, tiles_per_stream, n_prompt_streams, n_sample_streams, sample_seq):
    s = pl.program_id(0)
    w = _Weights(ng_ref, win_ref, ps_ref, dwb_ref, lng_ref, lnb_ref, pww_ref, pwb_ref, wout_ref, fg_ref,
                 bd_ref, dwu_ref)

    @pl.when(s == 0)
    def _():
        copies = _start_weight_copies(win_hbm, pww_hbm, wout_hbm, win_ref, pww_ref, wout_ref, copy_sem)
        _prepare_weights(pmix_ref, dww_ref, bd_ref, dwu_ref)
        _sample_rows(n_sample_streams, sample_seq, xs_ref, cp_ref, cc_ref, w, lambda i: copies[i].wait(),
                     ys_ref, sps_ref, scs_ref, uext_ref, vext_ref, pk_ref, pooled_ref, act_ref, cat_ref)

    @pl.when(s > 0)
    def _():
        steps_per_stream = tiles_per_stream // TILES_PER_STEP
        t = s - 1

        for i in range(TILES_PER_STEP):
            rows = pl.ds(i * TIME_TILE, TIME_TILE)
            _prompt_tile(t // steps_per_stream, lax.rem(t, steps_per_stream) * TILES_PER_STEP + i,
                         tiles_per_stream - 1, n_prompt_streams, i == 0, i == TILES_PER_STEP - 1,
                         xp_ref.at[rows], w, yp_ref.at[rows],
                         spp_ref, scp_ref, uext_ref, vext_ref, pk_ref, pooled_ref, act_ref, cat_ref)


def _resident(shape):
    return pl.BlockSpec(shape, lambda s: (0,) * len(shape), pipeline_mode=pl.Buffered(1))


def _whole(shape):
    return pl.BlockSpec(shape, lambda s: (0,) * len(shape))


def kernel(x_prompt, x_sample, cache_pool, cache_conv, norm_g, w_in, pool_mix, pool_scale, dw_w, dw_b,
           ln_g, ln_b, pw_w, pw_b, w_out, final_g):
    batch, seq, d_model = x_prompt.shape
    dec_batch, dec_seq, _ = x_sample.shape
    assert d_model == D_MODEL and w_in.shape == (1, D_MODEL, D_IN)
    assert pw_w.shape == (1, D_CONV, D_CONV) and w_out.shape == (1, D_MODEL, D_MODEL)
    step_rows = TILES_PER_STEP * TIME_TILE
    assert seq % step_rows == 0 and dec_seq % BF16_ROWS == 0 and dec_batch * dec_seq <= TIME_TILE
    tiles_per_stream = seq // TIME_TILE
    steps_per_stream = seq // step_rows
    n_steps = batch * steps_per_stream
    rows = dec_batch * dec_seq

    def step_index(s):
        t = jnp.maximum(s - 1, 0)
        return (t // steps_per_stream, t % steps_per_stream, 0)

    in_hbm = pl.BlockSpec(memory_space=pl.ANY)

    y_prompt, y_sample, sp_prompt, sc_prompt, sp_sample, sc_sample = pl.pallas_call(
        functools.partial(_mixer_kernel, tiles_per_stream=tiles_per_stream,
                          n_prompt_streams=batch, n_sample_streams=dec_batch, sample_seq=dec_seq),
        grid=(1 + n_steps,),
        in_specs=[
            pl.BlockSpec((None, step_rows, D_MODEL), step_index),
            _whole((rows, D_MODEL)),
            _whole((POOL_HIST, dec_batch, D_POOL)),
            _whole((CONV_HIST, dec_batch, D_CONV)),
            _resident((1, D_MODEL)),
            in_hbm,
            _resident((len(POOL_WINDOWS), POOL_GROUP, POOL_GROUP)),
            _resident((1, D_POOL)),
            _resident((CONV_WIDTH, D_CONV)),
            _resident((1, D_CONV)),
            _resident((1, D_CONV)),
            _resident((1, D_CONV)),
            in_hbm,
            _resident((1, D_CONV)),
            in_hbm,
            _resident((1, D_MODEL)),
        ],
        out_specs=[
            pl.BlockSpec((None, step_rows, D_MODEL), step_index),
            _whole((rows, D_MODEL)),
            _whole((POOL_HIST, batch, D_POOL)),
            _whole((CONV_HIST, batch, D_CONV)),
            _whole((POOL_HIST, dec_batch, D_POOL)),
            _whole((CONV_HIST, dec_batch, D_CONV)),
        ],
        out_shape=[
            jax.ShapeDtypeStruct((batch, seq, D_MODEL), F32),
            jax.ShapeDtypeStruct((rows, D_MODEL), F32),
            jax.ShapeDtypeStruct((POOL_HIST, batch, D_POOL), F32),
            jax.ShapeDtypeStruct((CONV_HIST, batch, D_CONV), F32),
            jax.ShapeDtypeStruct((POOL_HIST, dec_batch, D_POOL), F32),
            jax.ShapeDtypeStruct((CONV_HIST, dec_batch, D_CONV), F32),
        ],
        scratch_shapes=[
            pltpu.VMEM((D_MODEL, D_IN), F32),
            pltpu.VMEM((D_CONV, D_CONV), F32),
            pltpu.VMEM((D_MODEL, D_MODEL), F32),
            pltpu.SemaphoreType.DMA((N_WEIGHT_COPIES,)),
            pltpu.VMEM((D_POOL // (2 * POOL_GROUP), 2 * POOL_GROUP, 2 * POOL_GROUP), BF16),
            pltpu.VMEM((CONV_WIDTH, D_CONV), U32),
            pltpu.VMEM((D_POOL // LANES, POOL_PAD + TIME_TILE, LANES), F32),
            pltpu.VMEM((D_CONV // LANES, CONV_PAD + TIME_TILE + SUBLANES, LANES), F32),
            pltpu.VMEM((2, D_CONV // LANES, (CONV_PAD + TIME_TILE) // 2, LANES), U32),
            pltpu.VMEM((TIME_TILE, D_POOL), BF16),
            pltpu.VMEM((TIME_TILE, D_CONV), BF16),
            pltpu.VMEM((TIME_TILE, D_MODEL), BF16),
        ],
        compiler_params=pltpu.CompilerParams(
            dimension_semantics=("arbitrary",), vmem_limit_bytes=VMEM_LIMIT_BYTES),
        name="stream_mixer",
    )(x_prompt, x_sample.reshape(rows, D_MODEL), jnp.swapaxes(cache_pool[0], 0, 1),
      jnp.swapaxes(cache_conv[0], 0, 1), norm_g, w_in, pool_mix[0], pool_scale, dw_w[0], dw_b, ln_g, ln_b,
      pw_w, pw_b, w_out, final_g.reshape(1, D_MODEL))

    def as_state(t):
        return jnp.swapaxes(t, 0, 1)[None]

    return (y_prompt, y_sample.reshape(dec_batch, dec_seq, D_MODEL), as_state(sp_prompt), as_state(sc_prompt),
            as_state(sp_sample), as_state(sc_sample))
```

```python
import functools

import jax
import jax.numpy as jnp
from jax import lax
from jax.experimental import pallas as pl
from jax.experimental.pallas import tpu as pltpu

D_MODEL = 1024
D_POOL = 512
D_CONV = 512
POOL_WINDOWS = (2, 4, 8, 16)
POOL_GROUP = 128
POOL_HIST = 15
CONV_WIDTH = 31
CONV_HIST = 30
D_IN = 2 * D_POOL + 3 * D_CONV
RMS_EPS = 1e-6
LN_EPS = 1e-5

LANES = 128
SUBLANES = 8
BF16_ROWS = 2 * SUBLANES
POOL_PAD = 16
CONV_PAD = 32
TIME_TILE = 1024
TILES_PER_STEP = 1
OUT_PIECES = 4
HEAD_PIECES = 2
ROW_CHUNK = 64
WIN_PIECE_ROWS = 256
N_WIN_PIECES = D_MODEL // WIN_PIECE_ROWS
N_WEIGHT_COPIES = N_WIN_PIECES + 2
VMEM_LIMIT_BYTES = 58 * 1024 * 1024

F32 = jnp.float32
BF16 = jnp.bfloat16
U32 = jnp.uint32


def _rms_norm(x, g):
    ms = jnp.mean(x * x, axis=-1, keepdims=True)
    return x * lax.rsqrt(ms + RMS_EPS) * g


def _silu(x):
    return x * jax.nn.sigmoid(x)


def _store_blocked(ext, row0, val):
    for cb in range(val.shape[-1] // LANES):
        ext[cb, pl.ds(row0, val.shape[0]), :] = val[:, cb * LANES:(cb + 1) * LANES]


def _load_blocked(ext, row0, rows):
    return jnp.concatenate([ext[cb, pl.ds(row0, rows), :] for cb in range(ext.shape[0])], axis=-1)


def _pool_rows(uext, row0, rows, cnt=None):
    outs = []
    for gi, w in enumerate(POOL_WINDOWS):
        cur = uext[gi, pl.ds(row0, rows), :]
        span = min(w, SUBLANES)
        acc = cur
        for i in range(1, span):
            acc = acc + uext[gi, pl.ds(row0 - i, rows), :]
        if w > span:
            assert w == 2 * span
            head = uext[gi, pl.ds(row0 - span, span), :]
            for i in range(1, span):
                head = head + uext[gi, pl.ds(row0 - span - i, span), :]
            acc = acc + jnp.concatenate([head, acc[:rows - span]], axis=0)
        if cnt is None:
            outs.append(acc * (1.0 / w) - cur)
        else:
            outs.append(acc / cnt(w) - cur)
    return jnp.concatenate(outs, axis=-1)


def _pack_rows(vext, pk, row0, rows):
    for cb in range(vext.shape[0]):
        for odd in (0, 1):
            x = vext[cb, pl.ds(row0 + odd, rows), :].astype(BF16)
            pk[odd, cb, pl.ds(row0 // 2, rows // 2), :] = pltpu.bitcast(x, U32)


def _conv_rows(pk, row0, rows, dwu_ref, dwb_ref):
    assert CONV_HIST % 2 == 0 and row0 % 2 == 0 and rows % BF16_ROWS == 0
    outs = []
    for cb in range(D_CONV // LANES):
        sl = slice(cb * LANES, (cb + 1) * LANES)
        part = [None] * 4
        for k in range(CONV_WIDTH):
            odd = k % 2
            start = (row0 + k - CONV_HIST - odd) // 2
            x = pltpu.bitcast(pk[odd, cb, pl.ds(start, rows // 2), :], BF16)
            w = pltpu.bitcast(jnp.broadcast_to(dwu_ref[k:k + 1, sl], (rows // 2, LANES)), BF16)
            part[k % 4] = x * w if part[k % 4] is None else part[k % 4] + x * w
        acc = (part[0] + part[1]) + (part[2] + part[3])
        outs.append(acc.astype(F32) + dwb_ref[:, sl])
    return jnp.concatenate(outs, axis=-1)


def _conv_post(c, lng, lnb):
    mu = jnp.mean(c, axis=-1, keepdims=True)
    xc = c - mu
    var = jnp.mean(xc * xc, axis=-1, keepdims=True)
    return _silu(xc * lax.rsqrt(var + LN_EPS) * lng + lnb)


def _dot(a, w):
    return lax.dot_general(a, w, (((1,), (0,)), ((), ())), preferred_element_type=F32)


def _prepare_weights(pmix_ref, dww_ref, bd_ref, dwu_ref):
    half = 2 * POOL_GROUP
    bd_ref[...] = jnp.zeros(bd_ref.shape, bd_ref.dtype)
    for g in range(len(POOL_WINDOWS)):
        lo = (g % 2) * POOL_GROUP
        bd_ref[g // 2, lo:lo + POOL_GROUP, lo:lo + POOL_GROUP] = pmix_ref[g].astype(BF16)
    assert bd_ref.shape == (D_POOL // half, half, half)
    for k in range(CONV_WIDTH):
        tile = jnp.broadcast_to(dww_ref[k:k + 1, :], (BF16_ROWS, D_CONV)).astype(BF16)
        dwu_ref[k:k + 1, :] = pltpu.bitcast(tile, U32)[0:1, :]


def _no_wait(copy_index):
    del copy_index


def _mix_and_project(x_ref, g_pool, g_conv, pooled_ref, act_ref, cat_ref, bd_ref, ps_ref, pww_ref, pwb_ref,
                     wout_ref, fg_ref, out_pieces=1, await_weight=_no_wait):
    half = 2 * POOL_GROUP
    t = x_ref.shape[0]
    head_pieces = min(out_pieces, HEAD_PIECES)
    assert t % out_pieces == 0 and out_pieces % head_pieces == 0
    per_head = out_pieces // head_pieces
    outs = []
    for hp in range(head_pieces):
        hrows = slice(hp * (t // head_pieces), (hp + 1) * (t // head_pieces))
        mix = jnp.concatenate([_dot(pooled_ref[hrows, c * half:(c + 1) * half], bd_ref[c])
                               for c in range(D_POOL // half)], axis=-1)
        cat_ref[hrows, 0:D_POOL] = (_silu(g_pool[hrows, :]) * (mix * ps_ref[...])).astype(BF16)
        if hp == 0:
            await_weight(N_WIN_PIECES)
        cmix = _dot(act_ref[hrows, :], pww_ref[...]) + pwb_ref[...]
        cat_ref[hrows, D_POOL:D_MODEL] = (_silu(g_conv[hrows, :]) * cmix).astype(BF16)
        if hp == 0:
            await_weight(N_WIN_PIECES + 1)
        for i in range(hp * per_head, (hp + 1) * per_head):
            rows = slice(i * (t // out_pieces), (i + 1) * (t // out_pieces))
            outs.append(_rms_norm(x_ref[rows, :] + _dot(cat_ref[rows, :], wout_ref[...]), fg_ref[...]))
    return jnp.concatenate(outs, axis=0)


def _project(x, ng_ref, win_ref):
    h = _rms_norm(x, ng_ref[...]).astype(BF16)

    def proj(lo, hi):
        return _dot(h, win_ref[:, lo:hi])

    u = proj(0, D_POOL)
    g_pool = proj(D_POOL, 2 * D_POOL)
    v = proj(2 * D_POOL, 2 * D_POOL + D_CONV) * jax.nn.sigmoid(proj(2 * D_POOL + D_CONV, 2 * D_POOL + 2 * D_CONV))
    g_conv = proj(2 * D_POOL + 2 * D_CONV, D_IN)
    return u, g_pool, v, g_conv


def _project_by_rows(x, ng_ref, win_ref, await_weight):
    h = _rms_norm(x, ng_ref[...]).astype(BF16)
    p = None
    for piece in range(N_WIN_PIECES):
        await_weight(piece)
        rows = slice(piece * WIN_PIECE_ROWS, (piece + 1) * WIN_PIECE_ROWS)
        part = _dot(h[:, rows], win_ref[rows, :])
        p = part if p is None else p + part
    u = p[:, 0:D_POOL]
    g_pool = p[:, D_POOL:2 * D_POOL]
    v = p[:, 2 * D_POOL:2 * D_POOL + D_CONV] * jax.nn.sigmoid(p[:, 2 * D_POOL + D_CONV:2 * D_POOL + 2 * D_CONV])
    g_conv = p[:, 2 * D_POOL + 2 * D_CONV:D_IN]
    return u, g_pool, v, g_conv


def _prompt_tile(b, j, last_j, n_streams, may_start, may_end, x_ref, w, y_ref, sp_ref, sc_ref,
                 uext_ref, vext_ref, pk_ref, pooled_ref, act_ref, cat_ref):
    tt = TIME_TILE

    if may_start:
        @pl.when(j == 0)
        def _():
            _store_blocked(uext_ref, 0, jnp.zeros((POOL_PAD, D_POOL), F32))
            _store_blocked(vext_ref, 0, jnp.zeros((CONV_PAD, D_CONV), F32))
            _store_blocked(vext_ref, CONV_PAD + tt, jnp.zeros((SUBLANES, D_CONV), F32))

    u, g_pool, v, g_conv = _project(x_ref[...], w.ng, w.win)
    _store_blocked(uext_ref, POOL_PAD, u)
    _store_blocked(vext_ref, CONV_PAD, v)

    row = lax.broadcasted_iota(jnp.int32, (ROW_CHUNK, LANES), 0)

    def first_cnt(win):
        return jnp.where(j == 0, jnp.minimum(row + 1, win), win).astype(F32)

    for r0 in range(0, tt, ROW_CHUNK):
        pooled = _pool_rows(uext_ref, POOL_PAD + r0, ROW_CHUNK, first_cnt if r0 == 0 and may_start else None)
        pooled_ref[r0:r0 + ROW_CHUNK, :] = pooled.astype(BF16)
        lo = 0 if r0 == 0 else CONV_PAD + r0
        _pack_rows(vext_ref, pk_ref, lo, CONV_PAD + r0 + ROW_CHUNK - lo)
        c = _conv_rows(pk_ref, CONV_PAD + r0, ROW_CHUNK, w.dwu, w.dwb)
        act_ref[r0:r0 + ROW_CHUNK, :] = _conv_post(c, w.lng[...], w.lnb[...]).astype(BF16)

    y_ref[...] = _mix_and_project(x_ref, g_pool, g_conv, pooled_ref, act_ref, cat_ref, w.bd, w.ps, w.pww, w.pwb,
                                  w.wout, w.fg, out_pieces=OUT_PIECES)

    for bb in range(n_streams if may_end else 0):
        @pl.when((j == last_j) & (b == bb))
        def _(bb=bb):
            sp_ref[:, bb, :] = _load_blocked(uext_ref, POOL_PAD + tt - POOL_HIST, POOL_HIST)
            sc_ref[:, bb, :] = _load_blocked(vext_ref, CONV_PAD + tt - CONV_HIST, CONV_HIST)

    _store_blocked(uext_ref, 0, _load_blocked(uext_ref, tt, POOL_PAD))
    _store_blocked(vext_ref, 0, _load_blocked(vext_ref, tt, CONV_PAD))


def _sample_rows(n_streams, seq, x_ref, cp_ref, cc_ref, w, await_weight, y_ref, sp_ref, sc_ref,
                 uext_ref, vext_ref, pk_ref, pooled_ref, act_ref, cat_ref):
    u, g_pool, v, g_conv = _project_by_rows(x_ref[...], w.ng, w.win, await_weight)

    for s in range(n_streams):
        rows = slice(s * seq, (s + 1) * seq)
        _store_blocked(uext_ref, POOL_PAD - POOL_HIST, cp_ref[:, s, :])
        _store_blocked(uext_ref, POOL_PAD, u[rows, :])
        _store_blocked(vext_ref, CONV_PAD - CONV_HIST, cc_ref[:, s, :])
        _store_blocked(vext_ref, CONV_PAD, v[rows, :])
        pooled_ref[rows, :] = _pool_rows(uext_ref, POOL_PAD, seq).astype(BF16)
        _store_blocked(vext_ref, CONV_PAD + seq, jnp.zeros((SUBLANES, D_CONV), F32))
        _pack_rows(vext_ref, pk_ref, CONV_PAD - CONV_HIST, CONV_HIST + seq)
        c = _conv_rows(pk_ref, CONV_PAD, seq, w.dwu, w.dwb)
        act_ref[rows, :] = _conv_post(c, w.lng[...], w.lnb[...]).astype(BF16)
        sp_ref[:, s, :] = _load_blocked(uext_ref, POOL_PAD + seq - POOL_HIST, POOL_HIST)
        sc_ref[:, s, :] = _load_blocked(vext_ref, CONV_PAD + seq - CONV_HIST, CONV_HIST)

    y_ref[...] = _mix_and_project(x_ref, g_pool, g_conv, pooled_ref.at[0:n_streams * seq], act_ref.at[0:n_streams * seq],
                                  cat_ref.at[0:n_streams * seq], w.bd, w.ps, w.pww, w.pwb, w.wout, w.fg,
                                  out_pieces=2, await_weight=await_weight)


class _Weights:
    def __init__(self, ng, win, ps, dwb, lng, lnb, pww, pwb, wout, fg, bd, dwu):
        self.ng, self.win, self.ps, self.dwb, self.lng, self.lnb = ng, win, ps, dwb, lng, lnb
        self.pww, self.pwb, self.wout, self.fg, self.bd, self.dwu = pww, pwb, wout, fg, bd, dwu


def _weight_copies(win_hbm, pww_hbm, wout_hbm, win_ref, pww_ref, wout_ref, sem):
    copies = []
    for piece in range(N_WIN_PIECES):
        rows = pl.ds(piece * WIN_PIECE_ROWS, WIN_PIECE_ROWS)
        copies.append(pltpu.make_async_copy(win_hbm.at[0, rows, :], win_ref.at[rows, :], sem.at[piece]))
    copies.append(pltpu.make_async_copy(pww_hbm.at[0], pww_ref, sem.at[N_WIN_PIECES]))
    copies.append(pltpu.make_async_copy(wout_hbm.at[0], wout_ref, sem.at[N_WIN_PIECES + 1]))
    assert len(copies) == N_WEIGHT_COPIES
    return copies


def _mixer_kernel(xp_ref, xs_ref, cp_ref, cc_ref, ng_ref, win_hbm, pmix_ref, ps_ref, dww_ref, dwb_ref,
                  lng_ref, lnb_ref, pww_hbm, pwb_ref, wout_hbm, fg_ref,
                  yp_ref, ys_ref, spp_ref, scp_ref, sps_ref, scs_ref,
                  win_ref, pww_ref, wout_ref, copy_sem,
                  bd_ref, dwu_ref, uext_ref, vext_ref, pk_ref, pooled_ref, act_ref, cat_ref,
                  *, tiles_per_stream, n_prompt_streams, n_sample_streams, sample_seq):
    s = pl.program_id(0)
    w = _Weights(ng_ref, win_ref, ps_ref, dwb_ref, lng_ref, lnb_ref, pww_ref, pwb_ref, wout_ref, fg_ref,
                 bd_ref, dwu_ref)

    @pl.when(s == 0)
    def _():
        copies = _weight_copies(win_hbm, pww_hbm, wout_hbm, win_ref, pww_ref, wout_ref, copy_sem)
        for copy in copies:
            copy.start()
        _prepare_weights(pmix_ref, dww_ref, bd_ref, dwu_ref)
        _sample_rows(n_sample_streams, sample_seq, xs_ref, cp_ref, cc_ref, w, lambda i: copies[i].wait(),
                     ys_ref, sps_ref, scs_ref, uext_ref, vext_ref, pk_ref, pooled_ref, act_ref, cat_ref)

    @pl.when(s > 0)
    def _():
        steps_per_stream = tiles_per_stream // TILES_PER_STEP
        t = s - 1

        for i in range(TILES_PER_STEP):
            rows = pl.ds(i * TIME_TILE, TIME_TILE)
            _prompt_tile(t // steps_per_stream, lax.rem(t, steps_per_stream) * TILES_PER_STEP + i,
                         tiles_per_stream - 1, n_prompt_streams, i == 0, i == TILES_PER_STEP - 1,
                         xp_ref.at[rows], w, yp_ref.at[rows],
                         spp_ref, scp_ref, uext_ref, vext_ref, pk_ref, pooled_ref, act_ref, cat_ref)


def _resident(shape):
    return pl.BlockSpec(shape, lambda s: (0,) * len(shape), pipeline_mode=pl.Buffered(1))


def _whole(shape):
    return pl.BlockSpec(shape, lambda s: (0,) * len(shape))


def kernel(x_prompt, x_sample, cache_pool, cache_conv, norm_g, w_in, pool_mix, pool_scale, dw_w, dw_b,
           ln_g, ln_b, pw_w, pw_b, w_out, final_g):
    batch, seq, d_model = x_prompt.shape
    dec_batch, dec_seq, _ = x_sample.shape
    assert d_model == D_MODEL and w_in.shape == (1, D_MODEL, D_IN)
    assert pw_w.shape == (1, D_CONV, D_CONV) and w_out.shape == (1, D_MODEL, D_MODEL)
    step_rows = TILES_PER_STEP * TIME_TILE
    assert seq % step_rows == 0 and dec_seq % BF16_ROWS == 0 and dec_batch * dec_seq <= TIME_TILE
    tiles_per_stream = seq // TIME_TILE
    steps_per_stream = seq // step_rows
    n_steps = batch * steps_per_stream
    rows = dec_batch * dec_seq

    def step_index(s):
        t = jnp.maximum(s - 1, 0)
        return (t // steps_per_stream, t % steps_per_stream, 0)

    in_hbm = pl.BlockSpec(memory_space=pl.ANY)

    y_prompt, y_sample, sp_prompt, sc_prompt, sp_sample, sc_sample = pl.pallas_call(
        functools.partial(_mixer_kernel, tiles_per_stream=tiles_per_stream,
                          n_prompt_streams=batch, n_sample_streams=dec_batch, sample_seq=dec_seq),
        grid=(1 + n_steps,),
        in_specs=[
            pl.BlockSpec((None, step_rows, D_MODEL), step_index),
            _whole((rows, D_MODEL)),
            _whole((POOL_HIST, dec_batch, D_POOL)),
            _whole((CONV_HIST, dec_batch, D_CONV)),
            _resident((1, D_MODEL)),
            in_hbm,
            _resident((len(POOL_WINDOWS), POOL_GROUP, POOL_GROUP)),
            _resident((1, D_POOL)),
            _resident((CONV_WIDTH, D_CONV)),
            _resident((1, D_CONV)),
            _resident((1, D_CONV)),
            _resident((1, D_CONV)),
            in_hbm,
            _resident((1, D_CONV)),
            in_hbm,
            _resident((1, D_MODEL)),
        ],
        out_specs=[
            pl.BlockSpec((None, step_rows, D_MODEL), step_index),
            _whole((rows, D_MODEL)),
            _whole((POOL_HIST, batch, D_POOL)),
            _whole((CONV_HIST, batch, D_CONV)),
            _whole((POOL_HIST, dec_batch, D_POOL)),
            _whole((CONV_HIST, dec_batch, D_CONV)),
        ],
        out_shape=[
            jax.ShapeDtypeStruct((batch, seq, D_MODEL), F32),
            jax.ShapeDtypeStruct((rows, D_MODEL), F32),
            jax.ShapeDtypeStruct((POOL_HIST, batch, D_POOL), F32),
            jax.ShapeDtypeStruct((CONV_HIST, batch, D_CONV), F32),
            jax.ShapeDtypeStruct((POOL_HIST, dec_batch, D_POOL), F32),
            jax.ShapeDtypeStruct((CONV_HIST, dec_batch, D_CONV), F32),
        ],
        scratch_shapes=[
            pltpu.VMEM((D_MODEL, D_IN), F32),
            pltpu.VMEM((D_CONV, D_CONV), F32),
            pltpu.VMEM((D_MODEL, D_MODEL), F32),
            pltpu.SemaphoreType.DMA((N_WEIGHT_COPIES,)),
            pltpu.VMEM((D_POOL // (2 * POOL_GROUP), 2 * POOL_GROUP, 2 * POOL_GROUP), BF16),
            pltpu.VMEM((CONV_WIDTH, D_CONV), U32),
            pltpu.VMEM((D_POOL // LANES, POOL_PAD + TIME_TILE, LANES), F32),
            pltpu.VMEM((D_CONV // LANES, CONV_PAD + TIME_TILE + SUBLANES, LANES), F32),
            pltpu.VMEM((2, D_CONV // LANES, (CONV_PAD + TIME_TILE) // 2, LANES), U32),
            pltpu.VMEM((TIME_TILE, D_POOL), BF16),
            pltpu.VMEM((TIME_TILE, D_CONV), BF16),
            pltpu.VMEM((TIME_TILE, D_MODEL), BF16),
        ],
        compiler_params=pltpu.CompilerParams(
            dimension_semantics=("arbitrary",), vmem_limit_bytes=VMEM_LIMIT_BYTES),
        name="stream_mixer",
    )(x_prompt, x_sample.reshape(rows, D_MODEL), jnp.swapaxes(cache_pool[0], 0, 1),
      jnp.swapaxes(cache_conv[0], 0, 1), norm_g, w_in, pool_mix[0], pool_scale, dw_w[0], dw_b, ln_g, ln_b,
      pw_w, pw_b, w_out, final_g.reshape(1, D_MODEL))

    def as_state(t):
        return jnp.swapaxes(t, 0, 1)[None]

    return (y_prompt, y_sample.reshape(dec_batch, dec_seq, D_MODEL), as_state(sp_prompt), as_state(sc_prompt),
            as_state(sp_sample), as_state(sc_sample))
```
